```python
import math
import jax, jax.numpy as jnp
from jax import lax
import numpy as np

D_MODEL = 1024
BATCH = 1
SEQ = 16384
DEPTH = 4
DEC_BATCH = 8
DEC_SEQ = 4096
PAST_LEN = 128

D_INNER = 2 * D_MODEL
SSD_HEAD_DIM = 64
SSD_HEADS = D_INNER // SSD_HEAD_DIM
SSD_GROUPS = 8
HEADS_PER_GROUP = SSD_HEADS // SSD_GROUPS
D_STATE = 128
CONV_WIDTH = 5
CONV_DIM = D_INNER + 2 * SSD_GROUPS * D_STATE
CHUNK = 128
ATTN_HEADS = 8
QK_DIM = D_MODEL // ATTN_HEADS // 2
V_DIM = 2 * QK_DIM
ATTN_WIDTH = ATTN_HEADS * V_DIM
Q_BLOCK = 128
ROPE_THETA = 10000.0
D_FF = ((8 * D_MODEL + 3 * 256 - 1) // (3 * 256)) * 256
Z_COLS = D_INNER
XBC_COLS = CONV_DIM
DT_COLS = 2 * SSD_HEADS
QK_COLS = ATTN_HEADS * 2 * QK_DIM
V_COLS = ATTN_WIDTH
IN_COLS = Z_COLS + XBC_COLS + DT_COLS + 2 * QK_COLS + V_COLS
IN_SPLITS = (Z_COLS,
             Z_COLS + XBC_COLS,
             Z_COLS + XBC_COLS + DT_COLS,
             Z_COLS + XBC_COLS + DT_COLS + QK_COLS,
             Z_COLS + XBC_COLS + DT_COLS + 2 * QK_COLS)
ALPHA = (2 * DEPTH) ** 0.25
BETA = (8 * DEPTH) ** -0.25
EPS = 1e-5

kernel_name = 'hybrid_ssd_diffattn_encoder'


def layer_norm(x, g, b):
    xf = x.astype(jnp.float32)
    mu = jnp.mean(xf, axis=-1, keepdims=True)
    var = jnp.mean(jnp.square(xf - mu), axis=-1, keepdims=True)
    return ((xf - mu) * lax.rsqrt(var + EPS) * g + b).astype(x.dtype)


def rms_norm(x, g):
    xf = x.astype(jnp.float32)
    return (xf * lax.rsqrt(jnp.mean(jnp.square(xf), axis=-1, keepdims=True) + EPS) * g).astype(x.dtype)


def rope_tables(l):
    inv = ROPE_THETA ** (-jnp.arange(0, QK_DIM, 2, dtype=jnp.float32) / QK_DIM)
    ang = jnp.arange(l, dtype=jnp.float32)[:, None] * inv[None, :]
    return jnp.cos(ang), jnp.sin(ang)


def apply_rope(t, cos, sin):
    c = cos[:, None, None, :].astype(t.dtype)
    s = sin[:, None, None, :].astype(t.dtype)
    t1, t2 = t[..., :QK_DIM // 2], t[..., QK_DIM // 2:]
    return jnp.concatenate([t1 * c - t2 * s, t1 * s + t2 * c], axis=-1)


def centred_conv(x, w, b):
    y = lax.conv_general_dilated(
        x, w[:, None, :].astype(x.dtype), window_strides=(1,),
        padding=[(CONV_WIDTH // 2, CONV_WIDTH // 2)],
        dimension_numbers=('NWC', 'WIO', 'NWC'), feature_group_count=CONV_DIM)
    return y + b


def ssd_chunked(xs, dt, a, bm, cm):
    b, l, _, _ = xs.shape
    c = l // CHUNK
    g, hg, n, p = SSD_GROUPS, HEADS_PER_GROUP, D_STATE, SSD_HEAD_DIM
    xdt = (xs * dt[..., None]).reshape(b, c, CHUNK, g, hg, p)
    a_cum = jnp.cumsum((dt * a).reshape(b, c, CHUNK, g, hg), axis=2)
    bc = bm.reshape(b, c, CHUNK, g, n)
    cc = cm.reshape(b, c, CHUNK, g, n)
    ac = jnp.transpose(a_cum, (0, 1, 3, 4, 2))
    seg = ac[..., :, None] - ac[..., None, :]
    causal = jnp.tril(jnp.ones((CHUNK, CHUNK), dtype=bool))
    decay = jnp.exp(jnp.where(causal, seg, -jnp.inf))
    cb = jnp.einsum('bclgn,bcsgn->bcgls', cc, bc)
    y_diag = jnp.einsum('bcghls,bcsghp->bclghp', cb[:, :, :, None] * decay, xdt)
    decay_to_end = jnp.exp(ac[..., -1:] - ac)
    xw = xdt * jnp.transpose(decay_to_end, (0, 1, 4, 2, 3))[..., None]
    states = jnp.einsum('bcsgn,bcsghp->bcghpn', bc, xw)
    chunk_decay = jnp.exp(ac[..., -1])

    def step(carry, inp):
        st, dec = inp
        return carry * dec[..., None, None] + st, carry

    init = jnp.zeros((b, g, hg, p, n), jnp.float32)
    _, states_in = lax.scan(step, init, (jnp.moveaxis(states, 1, 0).astype(jnp.float32),
                                         jnp.moveaxis(chunk_decay, 1, 0)))
    states_in = jnp.moveaxis(states_in, 0, 1)
    y_off = jnp.einsum('bclgn,bcghpn->bclghp', cc, states_in) * jnp.exp(a_cum)[..., None]
    return (y_diag + y_off).reshape(b, l, SSD_HEADS, p)


def ssd_branch(xbc, z, dt_raw, conv_w, conv_b, a_log, dt_bias, d_skip, norm_g):
    b, l, _ = xbc.shape
    xbc = jax.nn.silu(centred_conv(xbc, conv_w, conv_b))
    xs, bm, cm = jnp.split(xbc, (D_INNER, D_INNER + SSD_GROUPS * D_STATE), axis=-1)
    xs = xs.reshape(b, l, SSD_HEADS, SSD_HEAD_DIM)
    bm = bm.reshape(b, l, SSD_GROUPS, D_STATE)
    cm = cm.reshape(b, l, SSD_GROUPS, D_STATE)
    dt = jax.nn.softplus(dt_raw.astype(jnp.float32).reshape(b, l, 2, SSD_HEADS) + dt_bias)
    a = -jnp.exp(a_log.astype(jnp.float32))
    flip = lambda t: jnp.flip(t, axis=1)
    y_fwd = ssd_chunked(xs, dt[:, :, 0], a[0], bm, cm)
    y_bwd = flip(ssd_chunked(flip(xs), flip(dt[:, :, 1]), a[1], flip(bm), flip(cm)))
    y = y_fwd + y_bwd + xs * d_skip[:, None]
    y = y.reshape(b, l, D_INNER) * jax.nn.silu(z.astype(jnp.float32))
    yg = y.reshape(b, l, SSD_GROUPS, D_INNER // SSD_GROUPS)
    yg = yg * lax.rsqrt(jnp.mean(jnp.square(yg), axis=-1, keepdims=True) + EPS)
    return (yg.reshape(b, l, D_INNER) * norm_g).astype(xbc.dtype)


def diff_attention(q, k, v, lam):
    b, l, h, _, d = q.shape
    nb = l // Q_BLOCK
    qb = jnp.moveaxis(q.reshape(b, nb, Q_BLOCK, h, 2, d), 1, 0)
    scale = d ** -0.5

    def block(qi):
        s = jnp.einsum('bqhcd,bkhcd->bhcqk', qi, k, preferred_element_type=jnp.float32) * scale
        pr = jax.nn.softmax(s, axis=-1)
        w = pr[:, :, 0] - lam * pr[:, :, 1]
        return jnp.einsum('bhqk,bkhv->bqhv', w.astype(v.dtype), v)

    out = lax.map(block, qb)
    return jnp.moveaxis(out, 0, 1).reshape(b, l, h, V_DIM)


def attn_branch(q, k, v, cos, sin, lam_qk, norm_g, lambda_init):
    b, l, _ = q.shape
    q = apply_rope(q.reshape(b, l, ATTN_HEADS, 2, QK_DIM), cos, sin)
    k = apply_rope(k.reshape(b, l, ATTN_HEADS, 2, QK_DIM), cos, sin)
    v = v.reshape(b, l, ATTN_HEADS, V_DIM)
    lq = lam_qk.astype(jnp.float32)
    lam = jnp.exp(jnp.sum(lq[0] * lq[1])) - jnp.exp(jnp.sum(lq[2] * lq[3])) + lambda_init
    o = diff_attention(q, k, v, lam)
    o = rms_norm(o, norm_g) * (1.0 - lambda_init)
    return o.reshape(b, l, ATTN_WIDTH)


def mixer(x, cos, sin, w_in, conv_w, conv_b, a_log, dt_bias, d_skip, ssd_norm_g, lam_qk, attn_norm_g,
          w_ssd_out, w_attn_out, w_gate, b_gate, w_out, lambda_init):
    z, xbc, dt_raw, q, k, v = jnp.split(x @ w_in, IN_SPLITS, axis=-1)
    ssd_out = ssd_branch(xbc, z, dt_raw, conv_w, conv_b, a_log, dt_bias, d_skip, ssd_norm_g) @ w_ssd_out
    attn_out = attn_branch(q, k, v, cos, sin, lam_qk, attn_norm_g, lambda_init) @ w_attn_out
    gates = jax.nn.sigmoid(x @ w_gate + b_gate)
    g_ssd, g_attn = jnp.split(gates, 2, axis=-1)
    return (g_ssd * ssd_out + g_attn * attn_out) @ w_out


def swiglu(x, w_up, w_down):
    a, g = jnp.split(x @ w_up, 2, axis=-1)
    return (jax.nn.silu(a) * g) @ w_down


def trunk(x, w_in, conv_w, conv_b, a_log, dt_bias, d_skip, ssd_norm_g, lam_qk, attn_norm_g,
          w_ssd_out, w_attn_out, w_gate, b_gate, w_out, ln1_g, ln1_b, w_ffn_up, w_ffn_down, ln2_g, ln2_b):
    cos, sin = rope_tables(x.shape[1])
    for i in range(DEPTH):
        lambda_init = 0.8 - 0.6 * math.exp(-0.3 * i)
        m = mixer(x, cos, sin, w_in[i], conv_w[i], conv_b[i], a_log[i], dt_bias[i], d_skip[i],
                  ssd_norm_g[i], lam_qk[i], attn_norm_g[i], w_ssd_out[i], w_attn_out[i],
                  w_gate[i], b_gate[i], w_out[i], lambda_init)
        x = layer_norm(ALPHA * x + m, ln1_g[i], ln1_b[i])
        f = swiglu(x, w_ffn_up[i], w_ffn_down[i])
        x = layer_norm(ALPHA * x + f, ln2_g[i], ln2_b[i])
    return x


def setup_inputs(seed: int = 0) -> dict:
    key = jax.random.key(seed)
    ks = jax.random.split(key, 24)
    nrm = lambda k, shape, scale: jax.random.normal(k, shape, jnp.float32) * scale
    x_prompt = nrm(ks[0], (BATCH, SEQ, D_MODEL), 1.0)
    x_sample = nrm(ks[1], (DEC_BATCH, DEC_SEQ, D_MODEL), 1.0)
    w_in = nrm(ks[2], (DEPTH, D_MODEL, IN_COLS), D_MODEL ** -0.5)
    conv_w = nrm(ks[3], (DEPTH, CONV_WIDTH, CONV_DIM), CONV_WIDTH ** -0.5)
    conv_b = nrm(ks[4], (DEPTH, CONV_DIM), 0.01)
    a_log = jnp.log(jax.random.uniform(ks[5], (DEPTH, 2, SSD_HEADS), jnp.float32, 1.0, 16.0))
    dt0 = jnp.exp(jax.random.uniform(ks[6], (DEPTH, 2, SSD_HEADS), jnp.float32,
                                     math.log(1e-3), math.log(1e-1)))
    dt_bias = dt0 + jnp.log(-jnp.expm1(-dt0))
    d_skip = 1.0 + nrm(ks[7], (DEPTH, SSD_HEADS), 0.02)
    ssd_norm_g = 1.0 + nrm(ks[8], (DEPTH, D_INNER), 0.02)
    lam_qk = nrm(ks[9], (DEPTH, 4, QK_DIM), 0.1)
    attn_norm_g = 1.0 + nrm(ks[10], (DEPTH, V_DIM), 0.02)
    w_ssd_out = nrm(ks[11], (DEPTH, D_INNER, D_MODEL), D_INNER ** -0.5)
    w_attn_out = nrm(ks[12], (DEPTH, ATTN_WIDTH, D_MODEL), ATTN_WIDTH ** -0.5)
    w_gate = nrm(ks[13], (DEPTH, D_MODEL, 2 * D_MODEL), D_MODEL ** -0.5)
    b_gate = nrm(ks[14], (DEPTH, 2 * D_MODEL), 0.01)
    w_out = nrm(ks[15], (DEPTH, D_MODEL, D_MODEL), D_MODEL ** -0.5 * BETA)
    ln1_g = 1.0 + nrm(ks[16], (DEPTH, D_MODEL), 0.02)
    ln1_b = nrm(ks[17], (DEPTH, D_MODEL), 0.01)
    w_ffn_up = nrm(ks[18], (DEPTH, D_MODEL, 2 * D_FF), D_MODEL ** -0.5)
    w_ffn_down = nrm(ks[19], (DEPTH, D_FF, D_MODEL), D_FF ** -0.5 * BETA)
    ln2_g = 1.0 + nrm(ks[20], (DEPTH, D_MODEL), 0.02)
    ln2_b = nrm(ks[21], (DEPTH, D_MODEL), 0.01)
    return {'x_prompt': x_prompt, 'x_sample': x_sample, 'w_in': w_in, 'conv_w': conv_w,
            'conv_b': conv_b, 'a_log': a_log, 'dt_bias': dt_bias, 'd_skip': d_skip,
            'ssd_norm_g': ssd_norm_g, 'lam_qk': lam_qk, 'attn_norm_g': attn_norm_g,
            'w_ssd_out': w_ssd_out, 'w_attn_out': w_attn_out, 'w_gate': w_gate, 'b_gate': b_gate,
            'w_out': w_out, 'ln1_g': ln1_g, 'ln1_b': ln1_b, 'w_ffn_up': w_ffn_up,
            'w_ffn_down': w_ffn_down, 'ln2_g': ln2_g, 'ln2_b': ln2_b}


def reference(x_prompt, x_sample, w_in, conv_w, conv_b, a_log, dt_bias, d_skip, ssd_norm_g, lam_qk,
              attn_norm_g, w_ssd_out, w_attn_out, w_gate, b_gate, w_out, ln1_g, ln1_b, w_ffn_up,
              w_ffn_down, ln2_g, ln2_b):
    y_prompt = trunk(x_prompt, w_in, conv_w, conv_b, a_log, dt_bias, d_skip, ssd_norm_g, lam_qk,
                     attn_norm_g, w_ssd_out, w_attn_out, w_gate, b_gate, w_out, ln1_g, ln1_b,
                     w_ffn_up, w_ffn_down, ln2_g, ln2_b)
    y_sample = trunk(x_sample, w_in, conv_w, conv_b, a_log, dt_bias, d_skip, ssd_norm_g, lam_qk,
                     attn_norm_g, w_ssd_out, w_attn_out, w_gate, b_gate, w_out, ln1_g, ln1_b,
                     w_ffn_up, w_ffn_down, ln2_g, ln2_b)
    return (y_prompt, y_sample)
```

```python
import functools
import math

import jax
import jax.numpy as jnp
from jax import lax
from jax.experimental import pallas as pl
from jax.experimental.pallas import tpu as pltpu

F32 = jnp.float32
BF16 = jnp.bfloat16

D_MODEL = 1024
DEPTH = 4
L_PROMPT = 16384
N_SAMPLE = 8
L_SAMPLE = 4096
T_PROMPT = L_PROMPT
T_TOTAL = T_PROMPT + N_SAMPLE * L_SAMPLE

D_INNER = 2 * D_MODEL
SSD_HEAD_DIM = 64
SSD_HEADS = D_INNER // SSD_HEAD_DIM
SSD_GROUPS = 8
D_STATE = 128
CONV_WIDTH = 5
BC_COLS = 2 * SSD_GROUPS * D_STATE
CONV_DIM = D_INNER + BC_COLS
CHUNK = 128
ATTN_HEADS = 8
QK_DIM = 64
V_DIM = 128
ATTN_WIDTH = ATTN_HEADS * V_DIM
ROPE_THETA = 10000.0
D_FF = 2816
ZX_COLS = D_INNER + CONV_DIM
DT_COLS = 2 * SSD_HEADS
QKV_OFF = ZX_COLS + DT_COLS
ALPHA = (2 * DEPTH) ** 0.25
EPS = 1e-5
QK_SCALE = QK_DIM ** -0.5

LANES = 128
SUBLANES = 8
VMEM_LIMIT = 56 * 1024 * 1024


def _cparams(sem):
    return pltpu.CompilerParams(dimension_semantics=sem, vmem_limit_bytes=VMEM_LIMIT)


def _seq_flags(t0, n):
    t1 = t0 + n
    first = (t0 == 0) | ((t0 >= T_PROMPT) & ((t0 - T_PROMPT) % L_SAMPLE == 0))
    last = (t1 == T_PROMPT) | ((t1 > T_PROMPT) & ((t1 - T_PROMPT) % L_SAMPLE == 0))
    return first, last


def _sigmoid(v):
    return 1.0 / (1.0 + jnp.exp(-v))


def _silu(v):
    return v * _sigmoid(v)


def _softplus(v):
    return jnp.maximum(v, 0.0) + jnp.log1p(jnp.exp(-jnp.abs(v)))


def _layer_norm(v, g, b):
    mu = jnp.mean(v, axis=-1, keepdims=True)
    d = v - mu
    var = jnp.mean(d * d, axis=-1, keepdims=True)
    return d * lax.rsqrt(var + EPS) * g + b


def _mm_kernel(x_ref, w_ref, o_ref):
    o_ref[...] = jnp.dot(x_ref[...], w_ref[...],
                         preferred_element_type=F32).astype(o_ref.dtype)


def _matmul(xb, w_stack, layer, tm, tn, out_dtype):
    t, k = xb.shape
    n = w_stack.shape[2]
    return pl.pallas_call(
        _mm_kernel,
        grid=(n // tn, t // tm),
        in_specs=[pl.BlockSpec((tm, k), lambda j, i: (i, 0)),
                  pl.BlockSpec((None, k, tn), lambda j, i: (layer, 0, j))],
        out_specs=pl.BlockSpec((tm, tn), lambda j, i: (i, j)),
        out_shape=jax.ShapeDtypeStruct((t, n), out_dtype),
        compiler_params=_cparams(("parallel", "parallel")),
        name="in_proj",
    )(xb, w_stack)


def _qkv_kernel(x_ref, w_ref, cos_ref, sin_ref, o_ref):
    j = pl.program_id(0)
    acc = jnp.dot(x_ref[...], w_ref[...], preferred_element_type=F32)
    tm = acc.shape[0]

    @pl.when(j == 2)
    def _():
        o_ref[...] = acc.astype(o_ref.dtype)

    @pl.when(j < 2)
    def _():
        scale = jnp.where(j == 0, QK_SCALE, 1.0).astype(F32)
        c = cos_ref[...] * scale
        s = sin_ref[...] * scale
        lane = lax.broadcasted_iota(jnp.int32, (tm, LANES), 1)
        first_half = (lane % QK_DIM) < (QK_DIM // 2)
        for b in range(ATTN_WIDTH // LANES):
            t = acc[:, b * LANES:(b + 1) * LANES]
            partner = jnp.where(first_half,
                                pltpu.roll(t, LANES - QK_DIM // 2, 1),
                                pltpu.roll(t, QK_DIM // 2, 1))
            o_ref[:, b * LANES:(b + 1) * LANES] = (t * c + partner * s).astype(o_ref.dtype)


def _qkv_proj(xb, w_stack, layer, cos_t, sin_t, tm):
    t, k = xb.shape
    n1 = T_PROMPT // tm
    n2 = L_SAMPLE // tm

    def pos_map(j, i):
        return (jnp.where(i < n1, i, (i - n1) % n2), 0)

    return pl.pallas_call(
        _qkv_kernel,
        grid=(3, t // tm),
        in_specs=[pl.BlockSpec((tm, k), lambda j, i: (i, 0)),
                  pl.BlockSpec((None, k, ATTN_WIDTH), lambda j, i: (layer, 0, j)),
                  pl.BlockSpec((tm, LANES), pos_map),
                  pl.BlockSpec((tm, LANES), pos_map)],
        out_specs=pl.BlockSpec((tm, ATTN_WIDTH), lambda j, i: (i, j)),
        out_shape=jax.ShapeDtypeStruct((t, 3 * ATTN_WIDTH), BF16),
        compiler_params=_cparams(("parallel", "parallel")),
        name="qkv_rope",
    )(xb, w_stack, cos_t, sin_t)


def _conv_kernel(x_ref, prev_ref, next_ref, w_ref, b_ref, o_ref, buf_ref, *, tm):
    i = pl.program_id(0)
    first, last = _seq_flags(i * tm, tm)
    halo = SUBLANES
    buf_ref[0:halo, :] = jnp.where(first, 0.0, prev_ref[...])
    buf_ref[halo:halo + tm, :] = x_ref[...]
    buf_ref[halo + tm:2 * halo + tm, :] = jnp.where(last, 0.0, next_ref[...])
    acc = b_ref[...] + w_ref[0:1, :] * buf_ref[pl.ds(halo - CONV_WIDTH // 2, tm), :]
    for k in range(1, CONV_WIDTH):
        acc = acc + w_ref[k:k + 1, :] * buf_ref[pl.ds(halo - CONV_WIDTH // 2 + k, tm), :]
    o_ref[...] = _silu(acc).astype(o_ref.dtype)


def _conv(zx, w_stack, b_stack, layer, col0, ncols, out_dtype, tm, tc):
    t = zx.shape[0]
    cb = col0 // tc
    pb = (col0 - D_INNER) // tc
    rb = tm // SUBLANES
    nrb = t // SUBLANES
    kern = functools.partial(_conv_kernel, tm=tm)
    return pl.pallas_call(
        kern,
        grid=(t // tm, ncols // tc),
        in_specs=[pl.BlockSpec((tm, tc), lambda i, j: (i, cb + j)),
                  pl.BlockSpec((SUBLANES, tc), lambda i, j: (jnp.maximum(i * rb - 1, 0), cb + j)),
                  pl.BlockSpec((SUBLANES, tc), lambda i, j: (jnp.minimum((i + 1) * rb, nrb - 1), cb + j)),
                  pl.BlockSpec((None, SUBLANES, tc), lambda i, j: (layer, 0, pb + j)),
                  pl.BlockSpec((None, 1, tc), lambda i, j: (layer, 0, pb + j))],
        out_specs=pl.BlockSpec((tm, tc), lambda i, j: (i, j)),
        out_shape=jax.ShapeDtypeStruct((t, ncols), out_dtype),
        scratch_shapes=[pltpu.VMEM((tm + 2 * SUBLANES, tc), F32)],
        compiler_params=_cparams(("parallel", "parallel")),
        name="conv_silu",
    )(zx, zx, zx, w_stack, b_stack)


def _ssd_kernel(*refs, reverse, final, n_chunks):
    if final:
        (xs_ref, bc_ref, dt_ref, alog_ref, dtb_ref, yf_ref, z_ref, dskip_ref, ng_ref,
         o_ref, st_ref) = refs
    else:
        xs_ref, bc_ref, dt_ref, alog_ref, dtb_ref, o_ref, st_ref = refs
    step = pl.program_id(0)
    c = (n_chunks - 1 - step) if reverse else step
    first, last = _seq_flags(c * CHUNK, CHUNK)

    @pl.when(last if reverse else first)
    def _():
        st_ref[...] = jnp.zeros_like(st_ref)

    q = CHUNK
    dt = _softplus(dt_ref[...] + dtb_ref[...])
    adt = dt * (-jnp.exp(alog_ref[...]))
    row = lax.broadcasted_iota(jnp.int32, (q, q), 0)
    col = lax.broadcasted_iota(jnp.int32, (q, q), 1)
    mask = (col >= row) if reverse else (col <= row)
    ac = jnp.dot(mask.astype(F32), adt, precision=lax.Precision.HIGHEST,
                 preferred_element_type=F32)
    ac_t = ac.T
    dt_t = dt.T
    lo = col < SSD_HEAD_DIM
    lo_row = lo[0:1, :]
    off = SSD_HEADS if reverse else 0
    end = 0 if reverse else q - 1
    pair = 2 * SSD_HEAD_DIM

    for g in range(SSD_GROUPS):
        bg = bc_ref[:, g * D_STATE:(g + 1) * D_STATE].astype(F32)
        cg = bc_ref[:, (SSD_GROUPS + g) * D_STATE:(SSD_GROUPS + g + 1) * D_STATE].astype(F32)
        bg_t = bg.T
        cb = jnp.dot(cg.astype(BF16), bg_t.astype(BF16), preferred_element_type=F32)
        y_parts = []
        for pr in range(2):
            h0 = g * 4 + pr * 2
            cols = slice(h0 * SSD_HEAD_DIM, h0 * SSD_HEAD_DIM + pair)
            xs_p = xs_ref[:, cols].astype(BF16)
            st_p = st_ref[:, cols]
            st_b = st_p.astype(BF16)
            zero = jnp.zeros_like(xs_p)
            x_lo = jnp.where(lo, xs_p, zero)
            x_hi = jnp.where(lo, zero, xs_p)
            s_lo = jnp.where(lo, st_b, zero)
            s_hi = jnp.where(lo, zero, st_b)
            m_parts, c_parts, b_parts, e_last = [], [], [], []
            for hh in range(2):
                ci = off + h0 + hh
                ac_col = ac[:, ci:ci + 1]
                ac_row = ac_t[ci:ci + 1, :]
                dt_row = dt_t[ci:ci + 1, :]
                decay = jnp.exp(jnp.where(mask, ac_col - ac_row, -jnp.inf))
                m_parts.append((cb * decay * dt_row).astype(BF16))
                c_parts.append((cg * jnp.exp(ac_col)).astype(BF16))
                a_last = ac_t[ci:ci + 1, end:end + 1]
                b_parts.append((bg_t * (jnp.exp(a_last - ac_row) * dt_row)).astype(BF16))
                e_last.append(jnp.exp(a_last))
            lhs_y = jnp.concatenate(m_parts + c_parts, axis=1)
            rhs_y = jnp.concatenate([x_lo, x_hi, s_lo, s_hi], axis=0)
            y_parts.append(jnp.dot(lhs_y, rhs_y, preferred_element_type=F32))
            lhs_s = jnp.concatenate(b_parts, axis=1)
            rhs_s = jnp.concatenate([x_lo, x_hi], axis=0)
            dec = jnp.where(lo_row, e_last[0], e_last[1])
            st_ref[:, cols] = st_p * dec + jnp.dot(lhs_s, rhs_s, preferred_element_type=F32)
        gcols = slice(g * 4 * SSD_HEAD_DIM, (g + 1) * 4 * SSD_HEAD_DIM)
        y_g = jnp.concatenate(y_parts, axis=1)
        if final:
            y_g = y_g + yf_ref[:, gcols] + xs_ref[:, gcols] * dskip_ref[:, gcols]
            y_g = y_g * _silu(z_ref[:, gcols])
            ms = jnp.mean(y_g * y_g, axis=-1, keepdims=True)
            o_ref[:, gcols] = (y_g * lax.rsqrt(ms + EPS) * ng_ref[:, gcols]).astype(o_ref.dtype)
        else:
            o_ref[:, gcols] = y_g


def _ssd(xs, bc, dt_raw, alog_p, dtb_p, layer, reverse, extra=None):
    t = xs.shape[0]
    nc = t // CHUNK
    final = extra is not None
    cm = (lambda s: (nc - 1 - s, 0)) if reverse else (lambda s: (s, 0))
    pm = lambda s: (layer, 0, 0)
    in_specs = [pl.BlockSpec((CHUNK, D_INNER), cm),
                pl.BlockSpec((CHUNK, BC_COLS), cm),
                pl.BlockSpec((CHUNK, LANES), cm),
                pl.BlockSpec((None, 1, LANES), pm),
                pl.BlockSpec((None, 1, LANES), pm)]
    args = [xs, bc, dt_raw, alog_p, dtb_p]
    if final:
        yf, zx, dskip_p, ng_p = extra
        in_specs += [pl.BlockSpec((CHUNK, D_INNER), cm),
                     pl.BlockSpec((CHUNK, D_INNER), cm),
                     pl.BlockSpec((None, 1, D_INNER), pm),
                     pl.BlockSpec((None, 1, D_INNER), pm)]
        args += [yf, zx, dskip_p, ng_p]
    kern = functools.partial(_ssd_kernel, reverse=reverse, final=final, n_chunks=nc)
    return pl.pallas_call(
        kern,
        grid=(nc,),
        in_specs=in_specs,
        out_specs=pl.BlockSpec((CHUNK, D_INNER), cm),
        out_shape=jax.ShapeDtypeStruct((t, D_INNER), BF16 if final else F32),
        scratch_shapes=[pltpu.VMEM((D_STATE, D_INNER), F32)],
        compiler_params=_cparams(("arbitrary",)),
        name="ssd_bwd" if reverse else "ssd_fwd",
    )(*args)


def _attn_kernel(lam_ref, g_ref, q_ref, k_ref, v_ref, o_ref, *, seq_len, tk, lambda_init):
    tq = q_ref.shape[0]
    qv = q_ref[...]
    lane = lax.broadcasted_iota(jnp.int32, (tq, LANES), 1)
    zero = jnp.zeros_like(qv)
    qq = jnp.concatenate([jnp.where(lane < QK_DIM, qv, zero),
                          jnp.where(lane < QK_DIM, zero, qv)], axis=0)

    def body(j, carry):
        m, l, acc = carry
        start = pl.multiple_of(j * tk, tk)
        kb = k_ref[pl.ds(start, tk), :]
        vb = v_ref[pl.ds(start, tk), :]
        s = lax.dot_general(qq, kb, (((1,), (1,)), ((), ())), preferred_element_type=F32)
        m_new = jnp.maximum(m, jnp.max(s, axis=-1, keepdims=True))
        alpha = jnp.exp(m - m_new)
        p = jnp.exp(s - m_new)
        l_new = alpha * l + jnp.sum(p, axis=-1, keepdims=True)
        acc_new = alpha * acc + jnp.dot(p.astype(BF16), vb, preferred_element_type=F32)
        return m_new, l_new, acc_new

    m0 = jnp.full((2 * tq, 1), -jnp.inf, F32)
    l0 = jnp.zeros((2 * tq, 1), F32)
    a0 = jnp.zeros((2 * tq, V_DIM), F32)
    _, l, acc = lax.fori_loop(0, seq_len // tk, body, (m0, l0, a0))
    o = acc / l
    lq = lam_ref[...]
    lam = (jnp.exp(jnp.sum(lq[0:1, :] * lq[1:2, :], axis=-1, keepdims=True))
           - jnp.exp(jnp.sum(lq[2:3, :] * lq[3:4, :], axis=-1, keepdims=True)) + lambda_init)
    d = o[:tq] - lam * o[tq:]
    ms = jnp.mean(d * d, axis=-1, keepdims=True)
    o_ref[...] = (d * lax.rsqrt(ms + EPS) * g_ref[...] * (1.0 - lambda_init)).astype(o_ref.dtype)


def _attention(qkv, lam_stack, g_stack, layer, tok_off, n_seq, seq_len, lambda_init, tq, tk):
    nq = seq_len // tq
    qb = tok_off // tq
    kb = tok_off // seq_len
    kern = functools.partial(_attn_kernel, seq_len=seq_len, tk=tk, lambda_init=lambda_init)
    return pl.pallas_call(
        kern,
        grid=(n_seq, ATTN_HEADS, nq),
        in_specs=[pl.BlockSpec((None, 4, QK_DIM), lambda s, h, i: (layer, 0, 0)),
                  pl.BlockSpec((None, 1, V_DIM), lambda s, h, i: (layer, 0, 0)),
                  pl.BlockSpec((tq, LANES), lambda s, h, i: (qb + s * nq + i, h)),
                  pl.BlockSpec((seq_len, LANES), lambda s, h, i: (kb + s, ATTN_HEADS + h)),
                  pl.BlockSpec((seq_len, LANES), lambda s, h, i: (kb + s, 2 * ATTN_HEADS + h))],
        out_specs=pl.BlockSpec((tq, V_DIM), lambda s, h, i: (s * nq + i, h)),
        out_shape=jax.ShapeDtypeStruct((n_seq * seq_len, ATTN_WIDTH), BF16),
        compiler_params=_cparams(("parallel", "parallel", "arbitrary")),
        name="diff_attn",
    )(lam_stack, g_stack, qkv, qkv, qkv)


def _post_kernel(x_ref, ssd_ref, att_ref, wg_ref, bg_ref, wso_ref, wao_ref, wo_ref,
                 g_ref, b_ref, o_ref, ob_ref):
    x = x_ref[...]
    gates = _sigmoid(jnp.dot(x.astype(BF16), wg_ref[...], preferred_element_type=F32) + bg_ref[...])
    so = jnp.dot(ssd_ref[...], wso_ref[...], preferred_element_type=F32)
    ao = jnp.dot(att_ref[...], wao_ref[...], preferred_element_type=F32)
    merged = gates[:, :D_MODEL] * so + gates[:, D_MODEL:] * ao
    m = jnp.dot(merged.astype(BF16), wo_ref[...], preferred_element_type=F32)
    y = _layer_norm(ALPHA * x + m, g_ref[...], b_ref[...])
    o_ref[...] = y
    ob_ref[...] = y.astype(BF16)


def _resident(shape, layer):
    nd = len(shape)
    return pl.BlockSpec((None,) + tuple(shape), lambda i: (layer,) + (0,) * nd,
                        pipeline_mode=pl.Buffered(1))


def _post(x, ssd_y, att_o, wg, bg, wso, wao, wo, g, b, layer, tm):
    t = x.shape[0]
    tok = lambda w: pl.BlockSpec((tm, w), lambda i: (i, 0))
    return pl.pallas_call(
        _post_kernel,
        grid=(t // tm,),
        in_specs=[tok(D_MODEL), tok(D_INNER), tok(ATTN_WIDTH),
                  _resident((D_MODEL, 2 * D_MODEL), layer), _resident((1, 2 * D_MODEL), layer),
                  _resident((D_INNER, D_MODEL), layer), _resident((ATTN_WIDTH, D_MODEL), layer),
                  _resident((D_MODEL, D_MODEL), layer),
                  _resident((1, D_MODEL), layer), _resident((1, D_MODEL), layer)],
        out_specs=[tok(D_MODEL), tok(D_MODEL)],
        out_shape=[jax.ShapeDtypeStruct((t, D_MODEL), F32),
                   jax.ShapeDtypeStruct((t, D_MODEL), BF16)],
        compiler_params=_cparams(("parallel",)),
        name="merge_out_ln",
    )(x, ssd_y, att_o, wg, bg, wso, wao, wo, g, b)


FF_SPLIT = 2
FF_CHUNK = D_FF // FF_SPLIT


def _ffn_kernel(x_ref, wu_ref, wd_ref, g_ref, b_ref, o_ref, ob_ref):
    x = x_ref[...]
    xb = x.astype(BF16)
    f = None
    for c in range(FF_SPLIT):
        a = jnp.dot(xb, wu_ref[:, c * FF_CHUNK:(c + 1) * FF_CHUNK], preferred_element_type=F32)
        u = jnp.dot(xb, wu_ref[:, D_FF + c * FF_CHUNK:D_FF + (c + 1) * FF_CHUNK],
                    preferred_element_type=F32)
        act = (_silu(a) * u).astype(BF16)
        part = jnp.dot(act, wd_ref[c * FF_CHUNK:(c + 1) * FF_CHUNK, :], preferred_element_type=F32)
        f = part if f is None else f + part
    y = _layer_norm(ALPHA * x + f, g_ref[...], b_ref[...])
    o_ref[...] = y
    ob_ref[...] = y.astype(BF16)


def _ffn(x, wu, wd, g, b, layer, tm):
    t = x.shape[0]
    tok = pl.BlockSpec((tm, D_MODEL), lambda i: (i, 0))
    return pl.pallas_call(
        _ffn_kernel,
        grid=(t // tm,),
        in_specs=[tok, _resident((D_MODEL, 2 * D_FF), layer), _resident((D_FF, D_MODEL), layer),
                  _resident((1, D_MODEL), layer), _resident((1, D_MODEL), layer)],
        out_specs=[tok, tok],
        out_shape=[jax.ShapeDtypeStruct((t, D_MODEL), F32),
                   jax.ShapeDtypeStruct((t, D_MODEL), BF16)],
        compiler_params=_cparams(("parallel",)),
        name="swiglu_ln",
    )(x, wu, wd, g, b)


def _pad_lanes(p):
    flat = p.reshape(DEPTH, 1, DT_COLS).astype(F32)
    return jnp.pad(flat, ((0, 0), (0, 0), (0, LANES - DT_COLS)))


def kernel(x_prompt, x_sample, w_in, conv_w, conv_b, a_log, dt_bias, d_skip, ssd_norm_g, lam_qk,
           attn_norm_g, w_ssd_out, w_attn_out, w_gate, b_gate, w_out, ln1_g, ln1_b, w_ffn_up,
           w_ffn_down, ln2_g, ln2_b):
    x = jnp.concatenate([x_prompt.reshape(T_PROMPT, D_MODEL),
                         x_sample.reshape(N_SAMPLE * L_SAMPLE, D_MODEL)], axis=0)
    xb = x.astype(BF16)

    inv = ROPE_THETA ** (-jnp.arange(0, QK_DIM, 2, dtype=F32) / QK_DIM)
    ang = jnp.arange(L_PROMPT, dtype=F32)[:, None] * inv[None, :]
    cos, sin = jnp.cos(ang), jnp.sin(ang)
    cos_t = jnp.concatenate([cos, cos, cos, cos], axis=-1)
    sin_t = jnp.concatenate([-sin, sin, -sin, sin], axis=-1)

    w_zx = w_in[:, :, :ZX_COLS].astype(BF16)
    w_dt = jnp.pad(w_in[:, :, ZX_COLS:QKV_OFF], ((0, 0), (0, 0), (0, LANES - DT_COLS))).astype(BF16)
    w_qkv = w_in[:, :, QKV_OFF:].astype(BF16)
    w_g = w_gate.astype(BF16)
    w_so = w_ssd_out.astype(BF16)
    w_ao = w_attn_out.astype(BF16)
    w_o = w_out.astype(BF16)
    w_up = w_ffn_up.astype(BF16)
    w_dn = w_ffn_down.astype(BF16)
    conv_w_p = jnp.pad(conv_w, ((0, 0), (0, SUBLANES - CONV_WIDTH), (0, 0)))
    conv_b_p = conv_b.reshape(DEPTH, 1, CONV_DIM)
    alog_p = _pad_lanes(a_log)
    dtb_p = _pad_lanes(dt_bias)
    dskip_p = jnp.repeat(d_skip, SSD_HEAD_DIM, axis=-1).reshape(DEPTH, 1, D_INNER)
    ng_p = ssd_norm_g.reshape(DEPTH, 1, D_INNER)
    ag_p = attn_norm_g.reshape(DEPTH, 1, V_DIM)
    row = lambda p: p.reshape(DEPTH, 1, -1)

    for i in range(DEPTH):
        lambda_init = 0.8 - 0.6 * math.exp(-0.3 * i)
        zx = _matmul(xb, w_zx, i, 1024, 2048, F32)
        dt_raw = _matmul(xb, w_dt, i, 2048, LANES, F32)
        qkv = _qkv_proj(xb, w_qkv, i, cos_t, sin_t, 1024)
        xs = _conv(zx, conv_w_p, conv_b_p, i, D_INNER, D_INNER, F32, 512, 512)
        bc = _conv(zx, conv_w_p, conv_b_p, i, 2 * D_INNER, BC_COLS, BF16, 512, 512)
        y_f = _ssd(xs, bc, dt_raw, alog_p, dtb_p, i, reverse=False)
        ssd_y = _ssd(xs, bc, dt_raw, alog_p, dtb_p, i, reverse=True,
                     extra=(y_f, zx, dskip_p, ng_p))
        att_p = _attention(qkv, lam_qk, ag_p, i, 0, 1, L_PROMPT, lambda_init, 256, 512)
        att_s = _attention(qkv, lam_qk, ag_p, i, T_PROMPT, N_SAMPLE, L_SAMPLE, lambda_init, 256, 512)
        att_o = jnp.concatenate([att_p, att_s], axis=0)
        x, xb = _post(x, ssd_y, att_o, w_g, row(b_gate), w_so, w_ao, w_o, row(ln1_g), row(ln1_b), i, 512)
        x, xb = _ffn(x, w_up, w_dn, row(ln2_g), row(ln2_b), i, 512)

    y_prompt = x[:T_PROMPT].reshape(1, L_PROMPT, D_MODEL)
    y_sample = x[T_PROMPT:].reshape(N_SAMPLE, L_SAMPLE, D_MODEL)
    return (y_prompt, y_sample)
```

```python
import functools
import math

import jax
import jax.numpy as jnp
from jax import lax
from jax.experimental import pallas as pl
from jax.experimental.pallas import tpu as pltpu

F32 = jnp.float32
BF16 = jnp.bfloat16

D_MODEL = 1024
DEPTH = 4
L_PROMPT = 16384
N_SAMPLE = 8
L_SAMPLE = 4096
T_PROMPT = L_PROMPT
T_TOTAL = T_PROMPT + N_SAMPLE * L_SAMPLE

D_INNER = 2 * D_MODEL
SSD_HEAD_DIM = 64
SSD_HEADS = D_INNER // SSD_HEAD_DIM
SSD_GROUPS = 8
D_STATE = 128
CONV_WIDTH = 5
BC_COLS = 2 * SSD_GROUPS * D_STATE
CONV_DIM = D_INNER + BC_COLS
CHUNK = 128
ATTN_HEADS = 8
QK_DIM = 64
V_DIM = 128
ATTN_WIDTH = ATTN_HEADS * V_DIM
ROPE_THETA = 10000.0
D_FF = 2816
ZX_COLS = D_INNER + CONV_DIM
DT_COLS = 2 * SSD_HEADS
QKV_OFF = ZX_COLS + DT_COLS
ALPHA = (2 * DEPTH) ** 0.25
EPS = 1e-5
QK_SCALE = QK_DIM ** -0.5
LOG2_E = math.log2(math.e)

LANES = 128
SUBLANES = 8
VMEM_LIMIT = 56 * 1024 * 1024


def _cparams(sem):
    return pltpu.CompilerParams(dimension_semantics=sem, vmem_limit_bytes=VMEM_LIMIT)


def _seq_flags(t0, n):
    t1 = t0 + n
    first = (t0 == 0) | ((t0 >= T_PROMPT) & ((t0 - T_PROMPT) % L_SAMPLE == 0))
    last = (t1 == T_PROMPT) | ((t1 > T_PROMPT) & ((t1 - T_PROMPT) % L_SAMPLE == 0))
    return first, last


def _sigmoid(v):
    return 1.0 / (1.0 + jnp.exp(-v))


def _silu(v):
    return v * _sigmoid(v)


def _softplus(v):
    return jnp.maximum(v, 0.0) + jnp.log1p(jnp.exp(-jnp.abs(v)))


def _layer_norm(v, g, b):
    mu = jnp.mean(v, axis=-1, keepdims=True)
    d = v - mu
    var = jnp.mean(d * d, axis=-1, keepdims=True)
    return d * lax.rsqrt(var + EPS) * g + b


def _mm_kernel(x_ref, w_ref, o_ref):
    o_ref[...] = jnp.dot(x_ref[...], w_ref[...],
                         preferred_element_type=F32).astype(o_ref.dtype)


def _matmul(xb, w_stack, layer, tm, tn, out_dtype):
    t, k = xb.shape
    n = w_stack.shape[2]
    return pl.pallas_call(
        _mm_kernel,
        grid=(n // tn, t // tm),
        in_specs=[pl.BlockSpec((tm, k), lambda j, i: (i, 0)),
                  pl.BlockSpec((None, k, tn), lambda j, i: (layer, 0, j))],
        out_specs=pl.BlockSpec((tm, tn), lambda j, i: (i, j)),
        out_shape=jax.ShapeDtypeStruct((t, n), out_dtype),
        compiler_params=_cparams(("parallel", "parallel")),
        name="in_proj",
    )(xb, w_stack)


ONES_ROWS = 16


def _head_proj_kernel(*refs, rope, scale, transpose, ones_rows):
    if rope:
        x_ref, w_ref, cos_ref, sin_ref, o_ref = refs
    else:
        x_ref, w_ref, o_ref = refs
    acc = jnp.dot(x_ref[...], w_ref[...], preferred_element_type=F32)
    tm = acc.shape[0]
    if rope:
        c = cos_ref[...] * scale
        s = sin_ref[...] * scale
        lane = lax.broadcasted_iota(jnp.int32, (tm, LANES), 1)
        first_half = (lane % QK_DIM) < (QK_DIM // 2)
    hrows = LANES + ones_rows
    for b in range(ATTN_WIDTH // LANES):
        t = acc[:, b * LANES:(b + 1) * LANES]
        if rope:
            partner = jnp.where(first_half,
                                pltpu.roll(t, LANES - QK_DIM // 2, 1),
                                pltpu.roll(t, QK_DIM // 2, 1))
            t = t * c + partner * s
        if transpose:
            o_ref[b * hrows:b * hrows + LANES, :] = t.T.astype(o_ref.dtype)
            if ones_rows:
                o_ref[b * hrows + LANES:(b + 1) * hrows, :] = jnp.ones((ones_rows, tm), o_ref.dtype)
        else:
            o_ref[:, b * LANES:(b + 1) * LANES] = t.astype(o_ref.dtype)


def _head_proj(xb, w_stack, layer, part, tm, rope_tables=None, scale=1.0, transpose=False,
               ones_rows=0):
    t, k = xb.shape
    n1 = T_PROMPT // tm
    n2 = L_SAMPLE // tm
    pos_map = lambda i: (jnp.where(i < n1, i, (i - n1) % n2), 0)
    in_specs = [pl.BlockSpec((tm, k), lambda i: (i, 0)),
                pl.BlockSpec((None, k, ATTN_WIDTH), lambda i: (layer, 0, part))]
    args = [xb, w_stack]
    if rope_tables is not None:
        in_specs += [pl.BlockSpec((tm, LANES), pos_map), pl.BlockSpec((tm, LANES), pos_map)]
        args += list(rope_tables)
    if transpose:
        rows = ATTN_HEADS * (LANES + ones_rows)
        out_spec = pl.BlockSpec((rows, tm), lambda i: (0, i))
        out_shape = jax.ShapeDtypeStruct((rows, t), BF16)
    else:
        out_spec = pl.BlockSpec((tm, ATTN_WIDTH), lambda i: (i, 0))
        out_shape = jax.ShapeDtypeStruct((t, ATTN_WIDTH), BF16)
    kern = functools.partial(_head_proj_kernel, rope=rope_tables is not None, scale=scale,
                             transpose=transpose, ones_rows=ones_rows)
    return pl.pallas_call(
        kern,
        grid=(t // tm,),
        in_specs=in_specs,
        out_specs=out_spec,
        out_shape=out_shape,
        compiler_params=_cparams(("parallel",)),
        name="head_proj",
    )(*args)


def _conv_kernel(x_ref, prev_ref, next_ref, w_ref, b_ref, o_ref, buf_ref, *, tm):
    i = pl.program_id(0)
    first, last = _seq_flags(i * tm, tm)
    halo = SUBLANES
    buf_ref[0:halo, :] = jnp.where(first, 0.0, prev_ref[...])
    buf_ref[halo:halo + tm, :] = x_ref[...]
    buf_ref[halo + tm:2 * halo + tm, :] = jnp.where(last, 0.0, next_ref[...])
    acc = b_ref[...] + w_ref[0:1, :] * buf_ref[pl.ds(halo - CONV_WIDTH // 2, tm), :]
    for k in range(1, CONV_WIDTH):
        acc = acc + w_ref[k:k + 1, :] * buf_ref[pl.ds(halo - CONV_WIDTH // 2 + k, tm), :]
    o_ref[...] = _silu(acc).astype(o_ref.dtype)


def _conv(zx, w_stack, b_stack, layer, col0, ncols, out_dtype, tm, tc):
    t = zx.shape[0]
    cb = col0 // tc
    pb = (col0 - D_INNER) // tc
    rb = tm // SUBLANES
    nrb = t // SUBLANES
    kern = functools.partial(_conv_kernel, tm=tm)
    return pl.pallas_call(
        kern,
        grid=(t // tm, ncols // tc),
        in_specs=[pl.BlockSpec((tm, tc), lambda i, j: (i, cb + j)),
                  pl.BlockSpec((SUBLANES, tc), lambda i, j: (jnp.maximum(i * rb - 1, 0), cb + j)),
                  pl.BlockSpec((SUBLANES, tc), lambda i, j: (jnp.minimum((i + 1) * rb, nrb - 1), cb + j)),
                  pl.BlockSpec((None, SUBLANES, tc), lambda i, j: (layer, 0, pb + j)),
                  pl.BlockSpec((None, 1, tc), lambda i, j: (layer, 0, pb + j))],
        out_specs=pl.BlockSpec((tm, tc), lambda i, j: (i, j)),
        out_shape=jax.ShapeDtypeStruct((t, ncols), out_dtype),
        scratch_shapes=[pltpu.VMEM((tm + 2 * SUBLANES, tc), F32)],
        compiler_params=_cparams(("parallel", "parallel")),
        name="conv_silu",
    )(zx, zx, zx, w_stack, b_stack)


def _ssd_kernel(*refs, reverse, final, n_chunks):
    if final:
        (xs_ref, bc_ref, dt_ref, alog_ref, dtb_ref, yf_ref, z_ref, dskip_ref, ng_ref,
         o_ref, st_ref) = refs
    else:
        xs_ref, bc_ref, dt_ref, alog_ref, dtb_ref, o_ref, st_ref = refs
    step = pl.program_id(0)
    c = (n_chunks - 1 - step) if reverse else step
    first, last = _seq_flags(c * CHUNK, CHUNK)

    @pl.when(last if reverse else first)
    def _():
        st_ref[...] = jnp.zeros_like(st_ref)

    q = CHUNK
    dt = _softplus(dt_ref[...] + dtb_ref[...])
    adt = dt * (-jnp.exp(alog_ref[...]))
    row = lax.broadcasted_iota(jnp.int32, (q, q), 0)
    col = lax.broadcasted_iota(jnp.int32, (q, q), 1)
    mask = (col >= row) if reverse else (col <= row)
    ac = jnp.dot(mask.astype(F32), adt, precision=lax.Precision.HIGHEST,
                 preferred_element_type=F32)
    ac_t = ac.T
    dt_t = dt.T
    lo = col < SSD_HEAD_DIM
    lo_row = lo[0:1, :]
    off = SSD_HEADS if reverse else 0
    end = 0 if reverse else q - 1
    pair = 2 * SSD_HEAD_DIM

    for g in range(SSD_GROUPS):
        bg = bc_ref[:, g * D_STATE:(g + 1) * D_STATE].astype(F32)
        cg = bc_ref[:, (SSD_GROUPS + g) * D_STATE:(SSD_GROUPS + g + 1) * D_STATE].astype(F32)
        bg_t = bg.T
        cb = jnp.dot(cg.astype(BF16), bg_t.astype(BF16), preferred_element_type=F32)
        y_parts = []
        for pr in range(2):
            h0 = g * 4 + pr * 2
            cols = slice(h0 * SSD_HEAD_DIM, h0 * SSD_HEAD_DIM + pair)
            xs_p = xs_ref[:, cols].astype(BF16)
            st_p = st_ref[:, cols]
            st_b = st_p.astype(BF16)
            zero = jnp.zeros_like(xs_p)
            x_lo = jnp.where(lo, xs_p, zero)
            x_hi = jnp.where(lo, zero, xs_p)
            s_lo = jnp.where(lo, st_b, zero)
            s_hi = jnp.where(lo, zero, st_b)
            m_parts, c_parts, b_parts, e_last = [], [], [], []
            for hh in range(2):
                ci = off + h0 + hh
                ac_col = ac[:, ci:ci + 1]
                ac_row = ac_t[ci:ci + 1, :]
                dt_row = dt_t[ci:ci + 1, :]
                decay = jnp.exp(jnp.where(mask, ac_col - ac_row, -jnp.inf))
                m_parts.append((cb * decay * dt_row).astype(BF16))
                c_parts.append((cg * jnp.exp(ac_col)).astype(BF16))
                a_last = ac_t[ci:ci + 1, end:end + 1]
                b_parts.append((bg_t * (jnp.exp(a_last - ac_row) * dt_row)).astype(BF16))
                e_last.append(jnp.exp(a_last))
            lhs_y = jnp.concatenate(m_parts + c_parts, axis=1)
            rhs_y = jnp.concatenate([x_lo, x_hi, s_lo, s_hi], axis=0)
            y_parts.append(jnp.dot(lhs_y, rhs_y, preferred_element_type=F32))
            lhs_s = jnp.concatenate(b_parts, axis=1)
            rhs_s = jnp.concatenate([x_lo, x_hi], axis=0)
            dec = jnp.where(lo_row, e_last[0], e_last[1])
            st_ref[:, cols] = st_p * dec + jnp.dot(lhs_s, rhs_s, preferred_element_type=F32)
        gcols = slice(g * 4 * SSD_HEAD_DIM, (g + 1) * 4 * SSD_HEAD_DIM)
        y_g = jnp.concatenate(y_parts, axis=1)
        if final:
            y_g = y_g + yf_ref[:, gcols] + xs_ref[:, gcols] * dskip_ref[:, gcols]
            y_g = y_g * _silu(z_ref[:, gcols])
            ms = jnp.mean(y_g * y_g, axis=-1, keepdims=True)
            o_ref[:, gcols] = (y_g * lax.rsqrt(ms + EPS) * ng_ref[:, gcols]).astype(o_ref.dtype)
        else:
            o_ref[:, gcols] = y_g


def _ssd(xs, bc, dt_raw, alog_p, dtb_p, layer, reverse, extra=None):
    t = xs.shape[0]
    nc = t // CHUNK
    final = extra is not None
    cm = (lambda s: (nc - 1 - s, 0)) if reverse else (lambda s: (s, 0))
    pm = lambda s: (layer, 0, 0)
    in_specs = [pl.BlockSpec((CHUNK, D_INNER), cm),
                pl.BlockSpec((CHUNK, BC_COLS), cm),
                pl.BlockSpec((CHUNK, LANES), cm),
                pl.BlockSpec((None, 1, LANES), pm),
                pl.BlockSpec((None, 1, LANES), pm)]
    args = [xs, bc, dt_raw, alog_p, dtb_p]
    if final:
        yf, zx, dskip_p, ng_p = extra
        in_specs += [pl.BlockSpec((CHUNK, D_INNER), cm),
                     pl.BlockSpec((CHUNK, D_INNER), cm),
                     pl.BlockSpec((None, 1, D_INNER), pm),
                     pl.BlockSpec((None, 1, D_INNER), pm)]
        args += [yf, zx, dskip_p, ng_p]
    kern = functools.partial(_ssd_kernel, reverse=reverse, final=final, n_chunks=nc)
    return pl.pallas_call(
        kern,
        grid=(nc,),
        in_specs=in_specs,
        out_specs=pl.BlockSpec((CHUNK, D_INNER), cm),
        out_shape=jax.ShapeDtypeStruct((t, D_INNER), BF16 if final else F32),
        scratch_shapes=[pltpu.VMEM((D_STATE, D_INNER), F32)],
        compiler_params=_cparams(("arbitrary",)),
        name="ssd_bwd" if reverse else "ssd_fwd",
    )(*args)


KB_PER_ITER = 4


def _attn_kernel(lam_ref, g_ref, qt_ref, k_ref, vt_ref, o_ref, sa_ref, sb_ref, *,
                 seq_len, tq, tk, lambda_init):
    nq = seq_len // tq
    kg = seq_len // tk // KB_PER_ITER
    rowi = lax.broadcasted_iota(jnp.int32, (LANES, tq), 0)

    def masked_q(qi):
        qt = qt_ref[:, pl.ds(pl.multiple_of(qi * tq, tq), tq)]
        zero = jnp.zeros_like(qt)
        return jnp.concatenate([jnp.where(rowi < QK_DIM, qt, zero),
                                jnp.where(rowi < QK_DIM, zero, qt)], axis=1)

    def scores(qqt, j, s_ref):
        start = pl.multiple_of(j * tk, tk)
        st = jnp.dot(k_ref[pl.ds(start, tk), :], qqt, preferred_element_type=F32)
        s_ref[...] = st
        return jnp.max(st, axis=0, keepdims=True)

    def update(j, s_ref, m, mblk, acc):
        start = pl.multiple_of(j * tk, tk)
        vext = vt_ref[:, pl.ds(start, tk)]
        m_new = jnp.maximum(m, mblk)
        alpha = jnp.exp2(m - m_new)
        pt = jnp.exp2((s_ref[...] - m_new).astype(BF16))
        return m_new, alpha * acc + jnp.dot(vext, pt, preferred_element_type=F32)

    def finalize(qi, acc):
        o = acc[:V_DIM, :] / acc[V_DIM:V_DIM + 1, :]
        lq = lam_ref[...]
        lam = (jnp.exp(jnp.sum(lq[0:1, :] * lq[1:2, :], axis=-1, keepdims=True))
               - jnp.exp(jnp.sum(lq[2:3, :] * lq[3:4, :], axis=-1, keepdims=True)) + lambda_init)
        d = o[:, :tq] - lam * o[:, tq:]
        ms = jnp.mean(d * d, axis=0, keepdims=True)
        dn = (d * lax.rsqrt(ms + EPS)).T
        o_ref[pl.ds(pl.multiple_of(qi * tq, tq), tq), :] = (
            dn * g_ref[...] * (1.0 - lambda_init)).astype(o_ref.dtype)

    def body(u, carry):
        m, mblk_a, acc = carry
        qi = u // kg
        g = u % kg
        first = g == 0
        last = g == kg - 1
        qqt = masked_q(qi)
        qqt_next = masked_q(jnp.where(last, jnp.minimum(qi + 1, nq - 1), qi))
        j0 = g * KB_PER_ITER
        m = jnp.where(first, -jnp.inf, m)
        bufs = (sa_ref, sb_ref)
        mblk = mblk_a
        for r in range(KB_PER_ITER):
            cur, nxt = bufs[r % 2], bufs[(r + 1) % 2]
            if r < KB_PER_ITER - 1:
                mblk_next = scores(qqt, j0 + r + 1, nxt)
            else:
                mblk_next = scores(qqt_next, jnp.where(last, 0, j0 + KB_PER_ITER), nxt)
            m, acc = update(j0 + r, cur, m, mblk, acc)
            mblk = mblk_next

        @pl.when(last)
        def _():
            finalize(qi, acc)

        return m, mblk, acc

    m0 = jnp.full((1, 2 * tq), -jnp.inf, F32)
    a0 = jnp.zeros((V_DIM + ONES_ROWS, 2 * tq), F32)
    lax.fori_loop(0, nq * kg, body, (m0, scores(masked_q(0), 0, sa_ref), a0))


def _attention(qt, k, vt, lam_stack, g_stack, layer, tok_off, n_seq, seq_len, lambda_init, tq, tk):
    sb = tok_off // seq_len
    kern = functools.partial(_attn_kernel, seq_len=seq_len, tq=tq, tk=tk, lambda_init=lambda_init)
    return pl.pallas_call(
        kern,
        grid=(n_seq, ATTN_HEADS),
        in_specs=[pl.BlockSpec((None, 4, QK_DIM), lambda s, h: (layer, 0, 0)),
                  pl.BlockSpec((None, 1, V_DIM), lambda s, h: (layer, 0, 0)),
                  pl.BlockSpec((LANES, seq_len), lambda s, h: (h, sb + s)),
                  pl.BlockSpec((seq_len, LANES), lambda s, h: (sb + s, h)),
                  pl.BlockSpec((V_DIM + ONES_ROWS, seq_len), lambda s, h: (h, sb + s))],
        out_specs=pl.BlockSpec((seq_len, V_DIM), lambda s, h: (s, h)),
        out_shape=jax.ShapeDtypeStruct((n_seq * seq_len, ATTN_WIDTH), BF16),
        scratch_shapes=[pltpu.VMEM((tk, 2 * tq), F32), pltpu.VMEM((tk, 2 * tq), F32)],
        compiler_params=_cparams(("parallel", "parallel")),
        name="diff_attn",
    )(lam_stack, g_stack, qt, k, vt)


def _post_kernel(x_ref, ssd_ref, att_ref, wg_ref, bg_ref, wso_ref, wao_ref, wo_ref,
                 g_ref, b_ref, o_ref, ob_ref):
    x = x_ref[...]
    gates = _sigmoid(jnp.dot(x.astype(BF16), wg_ref[...], preferred_element_type=F32) + bg_ref[...])
    so = jnp.dot(ssd_ref[...], wso_ref[...], preferred_element_type=F32)
    ao = jnp.dot(att_ref[...], wao_ref[...], preferred_element_type=F32)
    merged = gates[:, :D_MODEL] * so + gates[:, D_MODEL:] * ao
    m = jnp.dot(merged.astype(BF16), wo_ref[...], preferred_element_type=F32)
    y = _layer_norm(ALPHA * x + m, g_ref[...], b_ref[...])
    o_ref[...] = y
    ob_ref[...] = y.astype(BF16)


def _resident(shape, layer):
    nd = len(shape)
    return pl.BlockSpec((None,) + tuple(shape), lambda i: (layer,) + (0,) * nd,
                        pipeline_mode=pl.Buffered(1))


def _post(x, ssd_y, att_o, wg, bg, wso, wao, wo, g, b, layer, tm):
    t = x.shape[0]
    tok = lambda w: pl.BlockSpec((tm, w), lambda i: (i, 0))
    return pl.pallas_call(
        _post_kernel,
        grid=(t // tm,),
        in_specs=[tok(D_MODEL), tok(D_INNER), tok(ATTN_WIDTH),
                  _resident((D_MODEL, 2 * D_MODEL), layer), _resident((1, 2 * D_MODEL), layer),
                  _resident((D_INNER, D_MODEL), layer), _resident((ATTN_WIDTH, D_MODEL), layer),
                  _resident((D_MODEL, D_MODEL), layer),
                  _resident((1, D_MODEL), layer), _resident((1, D_MODEL), layer)],
        out_specs=[tok(D_MODEL), tok(D_MODEL)],
        out_shape=[jax.ShapeDtypeStruct((t, D_MODEL), F32),
                   jax.ShapeDtypeStruct((t, D_MODEL), BF16)],
        compiler_params=_cparams(("parallel",)),
        name="merge_out_ln",
    )(x, ssd_y, att_o, wg, bg, wso, wao, wo, g, b)


FF_SPLIT = 2
FF_CHUNK = D_FF // FF_SPLIT


def _ffn_kernel(x_ref, wu_ref, wd_ref, g_ref, b_ref, o_ref, ob_ref):
    x = x_ref[...]
    xb = x.astype(BF16)
    f = None
    for c in range(FF_SPLIT):
        a = jnp.dot(xb, wu_ref[:, c * FF_CHUNK:(c + 1) * FF_CHUNK], preferred_element_type=F32)
        u = jnp.dot(xb, wu_ref[:, D_FF + c * FF_CHUNK:D_FF + (c + 1) * FF_CHUNK],
                    preferred_element_type=F32)
        act = (_silu(a) * u).astype(BF16)
        part = jnp.dot(act, wd_ref[c * FF_CHUNK:(c + 1) * FF_CHUNK, :], preferred_element_type=F32)
        f = part if f is None else f + part
    y = _layer_norm(ALPHA * x + f, g_ref[...], b_ref[...])
    o_ref[...] = y
    ob_ref[...] = y.astype(BF16)


def _ffn(x, wu, wd, g, b, layer, tm):
    t = x.shape[0]
    tok = pl.BlockSpec((tm, D_MODEL), lambda i: (i, 0))
    return pl.pallas_call(
        _ffn_kernel,
        grid=(t // tm,),
        in_specs=[tok, _resident((D_MODEL, 2 * D_FF), layer), _resident((D_FF, D_MODEL), layer),
                  _resident((1, D_MODEL), layer), _resident((1, D_MODEL), layer)],
        out_specs=[tok, tok],
        out_shape=[jax.ShapeDtypeStruct((t, D_MODEL), F32),
                   jax.ShapeDtypeStruct((t, D_MODEL), BF16)],
        compiler_params=_cparams(("parallel",)),
        name="swiglu_ln",
    )(x, wu, wd, g, b)


def _pad_lanes(p):
    flat = p.reshape(DEPTH, 1, DT_COLS).astype(F32)
    return jnp.pad(flat, ((0, 0), (0, 0), (0, LANES - DT_COLS)))


def kernel(x_prompt, x_sample, w_in, conv_w, conv_b, a_log, dt_bias, d_skip, ssd_norm_g, lam_qk,
           attn_norm_g, w_ssd_out, w_attn_out, w_gate, b_gate, w_out, ln1_g, ln1_b, w_ffn_up,
           w_ffn_down, ln2_g, ln2_b):
    x = jnp.concatenate([x_prompt.reshape(T_PROMPT, D_MODEL),
                         x_sample.reshape(N_SAMPLE * L_SAMPLE, D_MODEL)], axis=0)
    xb = x.astype(BF16)

    inv = ROPE_THETA ** (-jnp.arange(0, QK_DIM, 2, dtype=F32) / QK_DIM)
    ang = jnp.arange(L_PROMPT, dtype=F32)[:, None] * inv[None, :]
    cos, sin = jnp.cos(ang), jnp.sin(ang)
    cos_t = jnp.concatenate([cos, cos, cos, cos], axis=-1)
    sin_t = jnp.concatenate([-sin, sin, -sin, sin], axis=-1)

    w_zx = w_in[:, :, :ZX_COLS].astype(BF16)
    w_dt = jnp.pad(w_in[:, :, ZX_COLS:QKV_OFF], ((0, 0), (0, 0), (0, LANES - DT_COLS))).astype(BF16)
    w_qkv = w_in[:, :, QKV_OFF:].astype(BF16)
    w_g = w_gate.astype(BF16)
    w_so = w_ssd_out.astype(BF16)
    w_ao = w_attn_out.astype(BF16)
    w_o = w_out.astype(BF16)
    w_up = w_ffn_up.astype(BF16)
    w_dn = w_ffn_down.astype(BF16)
    conv_w_p = jnp.pad(conv_w, ((0, 0), (0, SUBLANES - CONV_WIDTH), (0, 0)))
    conv_b_p = conv_b.reshape(DEPTH, 1, CONV_DIM)
    alog_p = _pad_lanes(a_log)
    dtb_p = _pad_lanes(dt_bias)
    dskip_p = jnp.repeat(d_skip, SSD_HEAD_DIM, axis=-1).reshape(DEPTH, 1, D_INNER)
    ng_p = ssd_norm_g.reshape(DEPTH, 1, D_INNER)
    ag_p = attn_norm_g.reshape(DEPTH, 1, V_DIM)
    row = lambda p: p.reshape(DEPTH, 1, -1)

    for i in range(DEPTH):
        lambda_init = 0.8 - 0.6 * math.exp(-0.3 * i)
        zx = _matmul(xb, w_zx, i, 1024, 2048, F32)
        dt_raw = _matmul(xb, w_dt, i, 2048, LANES, F32)
        q_t = _head_proj(xb, w_qkv, i, 0, 1024, rope_tables=(cos_t, sin_t),
                         scale=QK_SCALE * LOG2_E, transpose=True)
        k_r = _head_proj(xb, w_qkv, i, 1, 1024, rope_tables=(cos_t, sin_t))
        v_t = _head_proj(xb, w_qkv, i, 2, 1024, transpose=True, ones_rows=ONES_ROWS)
        xs = _conv(zx, conv_w_p, conv_b_p, i, D_INNER, D_INNER, F32, 512, 512)
        bc = _conv(zx, conv_w_p, conv_b_p, i, 2 * D_INNER, BC_COLS, BF16, 512, 512)
        y_f = _ssd(xs, bc, dt_raw, alog_p, dtb_p, i, reverse=False)
        ssd_y = _ssd(xs, bc, dt_raw, alog_p, dtb_p, i, reverse=True,
                     extra=(y_f, zx, dskip_p, ng_p))
        att_p = _attention(q_t, k_r, v_t, lam_qk, ag_p, i, 0, 1, L_PROMPT, lambda_init, 256, 512)
        att_s = _attention(q_t, k_r, v_t, lam_qk, ag_p, i, T_PROMPT, N_SAMPLE, L_SAMPLE,
                           lambda_init, 256, 512)
        att_o = jnp.concatenate([att_p, att_s], axis=0)
        x, xb = _post(x, ssd_y, att_o, w_g, row(b_gate), w_so, w_ao, w_o, row(ln1_g), row(ln1_b), i, 512)
        x, xb = _ffn(x, w_up, w_dn, row(ln2_g), row(ln2_b), i, 512)

    y_prompt = x[:T_PROMPT].reshape(1, L_PROMPT, D_MODEL)
    y_sample = x[T_PROMPT:].reshape(N_SAMPLE, L_SAMPLE, D_MODEL)
    return (y_prompt, y_sample)
```

```python
import functools
import math

import jax
import jax.numpy as jnp
from jax import lax
from jax.experimental import pallas as pl
from jax.experimental.pallas import tpu as pltpu

F32 = jnp.float32
BF16 = jnp.bfloat16

D_MODEL = 1024
DEPTH = 4
L_PROMPT = 16384
N_SAMPLE = 8
L_SAMPLE = 4096
T_PROMPT = L_PROMPT
T_TOTAL = T_PROMPT + N_SAMPLE * L_SAMPLE

D_INNER = 2 * D_MODEL
SSD_HEAD_DIM = 64
SSD_HEADS = D_INNER // SSD_HEAD_DIM
SSD_GROUPS = 8
D_STATE = 128
CONV_WIDTH = 5
BC_COLS = 2 * SSD_GROUPS * D_STATE
CONV_DIM = D_INNER + BC_COLS
CHUNK = 128
ATTN_HEADS = 8
QK_DIM = 64
V_DIM = 128
ATTN_WIDTH = ATTN_HEADS * V_DIM
ROPE_THETA = 10000.0
D_FF = 2816
ZX_COLS = D_INNER + CONV_DIM
DT_COLS = 2 * SSD_HEADS
QKV_OFF = ZX_COLS + DT_COLS
ALPHA = (2 * DEPTH) ** 0.25
EPS = 1e-5
QK_SCALE = QK_DIM ** -0.5
LOG2_E = math.log2(math.e)

LANES = 128
SUBLANES = 8
VMEM_LIMIT = 56 * 1024 * 1024


def _cparams(sem):
    return pltpu.CompilerParams(dimension_semantics=sem, vmem_limit_bytes=VMEM_LIMIT)


def _seq_flags(t0, n):
    t1 = t0 + n
    first = (t0 == 0) | ((t0 >= T_PROMPT) & ((t0 - T_PROMPT) % L_SAMPLE == 0))
    last = (t1 == T_PROMPT) | ((t1 > T_PROMPT) & ((t1 - T_PROMPT) % L_SAMPLE == 0))
    return first, last


def _sigmoid(v):
    return 1.0 / (1.0 + jnp.exp(-v))


def _silu(v):
    return v * _sigmoid(v)


def _softplus(v):
    return jnp.maximum(v, 0.0) + jnp.log1p(jnp.exp(-jnp.abs(v)))


def _layer_norm(v, g, b):
    mu = jnp.mean(v, axis=-1, keepdims=True)
    d = v - mu
    var = jnp.mean(d * d, axis=-1, keepdims=True)
    return d * lax.rsqrt(var + EPS) * g + b


def _mm_kernel(x_ref, w_ref, o_ref):
    o_ref[...] = jnp.dot(x_ref[...], w_ref[...],
                         preferred_element_type=F32).astype(o_ref.dtype)


def _matmul(xb, w_stack, layer, tm, tn, out_dtype):
    t, k = xb.shape
    n = w_stack.shape[2]
    return pl.pallas_call(
        _mm_kernel,
        grid=(n // tn, t // tm),
        in_specs=[pl.BlockSpec((tm, k), lambda j, i: (i, 0)),
                  pl.BlockSpec((None, k, tn), lambda j, i: (layer, 0, j))],
        out_specs=pl.BlockSpec((tm, tn), lambda j, i: (i, j)),
        out_shape=jax.ShapeDtypeStruct((t, n), out_dtype),
        compiler_params=_cparams(("parallel", "parallel")),
        name="in_proj",
    )(xb, w_stack)


ONES_ROWS = 16


def _head_proj_kernel(*refs, rope, scale, transpose, ones_rows):
    if rope:
        x_ref, w_ref, cos_ref, sin_ref, o_ref = refs
    else:
        x_ref, w_ref, o_ref = refs
    acc = jnp.dot(x_ref[...], w_ref[...], preferred_element_type=F32)
    tm = acc.shape[0]
    if rope:
        c = cos_ref[...] * scale
        s = sin_ref[...] * scale
        lane = lax.broadcasted_iota(jnp.int32, (tm, LANES), 1)
        first_half = (lane % QK_DIM) < (QK_DIM // 2)
    hrows = LANES + ones_rows
    for b in range(ATTN_WIDTH // LANES):
        t = acc[:, b * LANES:(b + 1) * LANES]
        if rope:
            partner = jnp.where(first_half,
                                pltpu.roll(t, LANES - QK_DIM // 2, 1),
                                pltpu.roll(t, QK_DIM // 2, 1))
            t = t * c + partner * s
        if transpose:
            o_ref[b * hrows:b * hrows + LANES, :] = t.T.astype(o_ref.dtype)
            if ones_rows:
                o_ref[b * hrows + LANES:(b + 1) * hrows, :] = jnp.ones((ones_rows, tm), o_ref.dtype)
        else:
            o_ref[:, b * LANES:(b + 1) * LANES] = t.astype(o_ref.dtype)


def _head_proj(xb, w_stack, layer, part, tm, rope_tables=None, scale=1.0, transpose=False,
               ones_rows=0):
    t, k = xb.shape
    n1 = T_PROMPT // tm
    n2 = L_SAMPLE // tm
    pos_map = lambda i: (jnp.where(i < n1, i, (i - n1) % n2), 0)
    in_specs = [pl.BlockSpec((tm, k), lambda i: (i, 0)),
                pl.BlockSpec((None, k, ATTN_WIDTH), lambda i: (layer, 0, part))]
    args = [xb, w_stack]
    if rope_tables is not None:
        in_specs += [pl.BlockSpec((tm, LANES), pos_map), pl.BlockSpec((tm, LANES), pos_map)]
        args += list(rope_tables)
    if transpose:
        rows = ATTN_HEADS * (LANES + ones_rows)
        out_spec = pl.BlockSpec((rows, tm), lambda i: (0, i))
        out_shape = jax.ShapeDtypeStruct((rows, t), BF16)
    else:
        out_spec = pl.BlockSpec((tm, ATTN_WIDTH), lambda i: (i, 0))
        out_shape = jax.ShapeDtypeStruct((t, ATTN_WIDTH), BF16)
    kern = functools.partial(_head_proj_kernel, rope=rope_tables is not None, scale=scale,
                             transpose=transpose, ones_rows=ones_rows)
    return pl.pallas_call(
        kern,
        grid=(t // tm,),
        in_specs=in_specs,
        out_specs=out_spec,
        out_shape=out_shape,
        compiler_params=_cparams(("parallel",)),
        name="head_proj",
    )(*args)


def _conv_kernel(x_ref, prev_ref, next_ref, w_ref, b_ref, o_ref, buf_ref, *, tm):
    i = pl.program_id(0)
    first, last = _seq_flags(i * tm, tm)
    halo = SUBLANES
    buf_ref[0:halo, :] = jnp.where(first, 0.0, prev_ref[...])
    buf_ref[halo:halo + tm, :] = x_ref[...]
    buf_ref[halo + tm:2 * halo + tm, :] = jnp.where(last, 0.0, next_ref[...])
    acc = b_ref[...] + w_ref[0:1, :] * buf_ref[pl.ds(halo - CONV_WIDTH // 2, tm), :]
    for k in range(1, CONV_WIDTH):
        acc = acc + w_ref[k:k + 1, :] * buf_ref[pl.ds(halo - CONV_WIDTH // 2 + k, tm), :]
    o_ref[...] = _silu(acc).astype(o_ref.dtype)


def _conv(zx, w_stack, b_stack, layer, col0, ncols, out_dtype, tm, tc):
    t = zx.shape[0]
    cb = col0 // tc
    pb = (col0 - D_INNER) // tc
    rb = tm // SUBLANES
    nrb = t // SUBLANES
    kern = functools.partial(_conv_kernel, tm=tm)
    return pl.pallas_call(
        kern,
        grid=(t // tm, ncols // tc),
        in_specs=[pl.BlockSpec((tm, tc), lambda i, j: (i, cb + j)),
                  pl.BlockSpec((SUBLANES, tc), lambda i, j: (jnp.maximum(i * rb - 1, 0), cb + j)),
                  pl.BlockSpec((SUBLANES, tc), lambda i, j: (jnp.minimum((i + 1) * rb, nrb - 1), cb + j)),
                  pl.BlockSpec((None, SUBLANES, tc), lambda i, j: (layer, 0, pb + j)),
                  pl.BlockSpec((None, 1, tc), lambda i, j: (layer, 0, pb + j))],
        out_specs=pl.BlockSpec((tm, tc), lambda i, j: (i, j)),
        out_shape=jax.ShapeDtypeStruct((t, ncols), out_dtype),
        scratch_shapes=[pltpu.VMEM((tm + 2 * SUBLANES, tc), F32)],
        compiler_params=_cparams(("parallel", "parallel")),
        name="conv_silu",
    )(zx, zx, zx, w_stack, b_stack)


def _ssd_kernel(*refs, reverse, final, n_chunks):
    if final:
        (xs_ref, bc_ref, dt_ref, alog_ref, dtb_ref, yf_ref, z_ref, dskip_ref, ng_ref,
         o_ref, st_ref) = refs
    else:
        xs_ref, bc_ref, dt_ref, alog_ref, dtb_ref, o_ref, st_ref = refs
    step = pl.program_id(0)
    c = (n_chunks - 1 - step) if reverse else step
    first, last = _seq_flags(c * CHUNK, CHUNK)

    @pl.when(last if reverse else first)
    def _():
        st_ref[...] = jnp.zeros_like(st_ref)

    q = CHUNK
    dt = _softplus(dt_ref[...] + dtb_ref[...])
    adt = dt * (-jnp.exp(alog_ref[...]) * LOG2_E)
    row = lax.broadcasted_iota(jnp.int32, (q, q), 0)
    col = lax.broadcasted_iota(jnp.int32, (q, q), 1)
    mask = (col >= row) if reverse else (col <= row)
    ac = jnp.dot(mask.astype(F32), adt, precision=lax.Precision.HIGHEST,
                 preferred_element_type=F32)
    ac_t = ac.T
    dt_t = dt.T
    lo = col < SSD_HEAD_DIM
    lo_row = lo[0:1, :]
    off = SSD_HEADS if reverse else 0
    end = 0 if reverse else q - 1
    pair = 2 * SSD_HEAD_DIM

    for g in range(SSD_GROUPS):
        bg = bc_ref[:, g * D_STATE:(g + 1) * D_STATE].astype(F32)
        cg = bc_ref[:, (SSD_GROUPS + g) * D_STATE:(SSD_GROUPS + g + 1) * D_STATE].astype(F32)
        bg_t = bg.T
        cb = jnp.dot(cg.astype(BF16), bg_t.astype(BF16), preferred_element_type=F32)
        y_parts = []
        for pr in range(2):
            h0 = g * 4 + pr * 2
            cols = slice(h0 * SSD_HEAD_DIM, h0 * SSD_HEAD_DIM + pair)
            xs_p = xs_ref[:, cols].astype(BF16)
            st_p = st_ref[:, cols]
            st_b = st_p.astype(BF16)
            zero = jnp.zeros_like(xs_p)
            x_lo = jnp.where(lo, xs_p, zero)
            x_hi = jnp.where(lo, zero, xs_p)
            s_lo = jnp.where(lo, st_b, zero)
            s_hi = jnp.where(lo, zero, st_b)
            m_parts, c_parts, b_parts, e_last = [], [], [], []
            for hh in range(2):
                ci = off + h0 + hh
                ac_col = jnp.broadcast_to(ac[:, ci:ci + 1], (q, q))
                ac_row = ac_t[ci:ci + 1, :]
                dt_row = dt_t[ci:ci + 1, :]
                decay = jnp.exp2(jnp.where(mask, ac_col - ac_row, -jnp.inf))
                m_parts.append((cb * decay * dt_row).astype(BF16))
                c_parts.append((cg * jnp.exp2(ac_col)).astype(BF16))
                a_last = ac_t[ci:ci + 1, end:end + 1]
                b_parts.append((bg_t * (jnp.exp2(a_last - ac_row) * dt_row)).astype(BF16))
                e_last.append(jnp.exp2(a_last))
            lhs_y = jnp.concatenate(m_parts + c_parts, axis=1)
            rhs_y = jnp.concatenate([x_lo, x_hi, s_lo, s_hi], axis=0)
            y_parts.append(jnp.dot(lhs_y, rhs_y, preferred_element_type=F32))
            lhs_s = jnp.concatenate(b_parts, axis=1)
            rhs_s = jnp.concatenate([x_lo, x_hi], axis=0)
            dec = jnp.where(lo_row, e_last[0], e_last[1])
            st_ref[:, cols] = st_p * dec + jnp.dot(lhs_s, rhs_s, preferred_element_type=F32)
        gcols = slice(g * 4 * SSD_HEAD_DIM, (g + 1) * 4 * SSD_HEAD_DIM)
        y_g = jnp.concatenate(y_parts, axis=1)
        if final:
            y_g = y_g + yf_ref[:, gcols] + xs_ref[:, gcols] * dskip_ref[:, gcols]
            y_g = y_g * _silu(z_ref[:, gcols])
            ms = jnp.mean(y_g * y_g, axis=-1, keepdims=True)
            o_ref[:, gcols] = (y_g * lax.rsqrt(ms + EPS) * ng_ref[:, gcols]).astype(o_ref.dtype)
        else:
            o_ref[:, gcols] = y_g


def _ssd(xs, bc, dt_raw, alog_p, dtb_p, layer, reverse, extra=None):
    t = xs.shape[0]
    nc = t // CHUNK
    final = extra is not None
    cm = (lambda s: (nc - 1 - s, 0)) if reverse else (lambda s: (s, 0))
    pm = lambda s: (layer, 0, 0)
    in_specs = [pl.BlockSpec((CHUNK, D_INNER), cm),
                pl.BlockSpec((CHUNK, BC_COLS), cm),
                pl.BlockSpec((CHUNK, LANES), cm),
                pl.BlockSpec((None, 1, LANES), pm),
                pl.BlockSpec((None, 1, LANES), pm)]
    args = [xs, bc, dt_raw, alog_p, dtb_p]
    if final:
        yf, zx, dskip_p, ng_p = extra
        in_specs += [pl.BlockSpec((CHUNK, D_INNER), cm),
                     pl.BlockSpec((CHUNK, D_INNER), cm),
                     pl.BlockSpec((None, 1, D_INNER), pm),
                     pl.BlockSpec((None, 1, D_INNER), pm)]
        args += [yf, zx, dskip_p, ng_p]
    kern = functools.partial(_ssd_kernel, reverse=reverse, final=final, n_chunks=nc)
    return pl.pallas_call(
        kern,
        grid=(nc,),
        in_specs=in_specs,
        out_specs=pl.BlockSpec((CHUNK, D_INNER), cm),
        out_shape=jax.ShapeDtypeStruct((t, D_INNER), BF16 if final else F32),
        scratch_shapes=[pltpu.VMEM((D_STATE, D_INNER), F32)],
        compiler_params=_cparams(("arbitrary",)),
        name="ssd_bwd" if reverse else "ssd_fwd",
    )(*args)


KB_PER_ITER = 4


def _attn_kernel(lam_ref, g_ref, qt_ref, k_ref, vt_ref, o_ref, sa_ref, sb_ref, *,
                 seq_len, tq, tk, lambda_init):
    nq = seq_len // tq
    kg = seq_len // tk // KB_PER_ITER
    rowi = lax.broadcasted_iota(jnp.int32, (LANES, tq), 0)

    def masked_q(qi):
        qt = qt_ref[:, pl.ds(pl.multiple_of(qi * tq, tq), tq)]
        zero = jnp.zeros_like(qt)
        return jnp.concatenate([jnp.where(rowi < QK_DIM, qt, zero),
                                jnp.where(rowi < QK_DIM, zero, qt)], axis=1)

    def scores(qqt, j, s_ref):
        start = pl.multiple_of(j * tk, tk)
        st = jnp.dot(k_ref[pl.ds(start, tk), :], qqt, preferred_element_type=F32)
        s_ref[...] = st
        return jnp.max(st, axis=0, keepdims=True)

    def update(j, s_ref, m, mblk, acc):
        start = pl.multiple_of(j * tk, tk)
        vext = vt_ref[:, pl.ds(start, tk)]
        m_new = jnp.maximum(m, mblk)
        alpha = jnp.exp2(m - m_new)
        pt = jnp.exp2(s_ref[...] - m_new).astype(BF16)
        return m_new, alpha * acc + jnp.dot(vext, pt, preferred_element_type=F32)

    def finalize(qi, acc):
        o = acc[:V_DIM, :] / acc[V_DIM:V_DIM + 1, :]
        lq = lam_ref[...]
        lam = (jnp.exp(jnp.sum(lq[0:1, :] * lq[1:2, :], axis=-1, keepdims=True))
               - jnp.exp(jnp.sum(lq[2:3, :] * lq[3:4, :], axis=-1, keepdims=True)) + lambda_init)
        d = o[:, :tq] - lam * o[:, tq:]
        ms = jnp.mean(d * d, axis=0, keepdims=True)
        dn = (d * lax.rsqrt(ms + EPS)).T
        o_ref[pl.ds(pl.multiple_of(qi * tq, tq), tq), :] = (
            dn * g_ref[...] * (1.0 - lambda_init)).astype(o_ref.dtype)

    def body(u, carry):
        m, mblk, acc = carry
        qi = u // kg
        g = u % kg
        first = g == 0
        last = g == kg - 1
        qqt = masked_q(qi)
        qqt_next = masked_q(jnp.where(last, jnp.minimum(qi + 1, nq - 1), qi))
        j0 = g * KB_PER_ITER
        m = jnp.where(first, -jnp.inf, m)
        bufs = (sa_ref, sb_ref)
        for r in range(KB_PER_ITER):
            cur, nxt = bufs[r % 2], bufs[(r + 1) % 2]
            if r < KB_PER_ITER - 1:
                mblk_next = scores(qqt, j0 + r + 1, nxt)
            else:
                mblk_next = scores(qqt_next, jnp.where(last, 0, j0 + KB_PER_ITER), nxt)
            m, acc = update(j0 + r, cur, m, mblk, acc)
            mblk = mblk_next

        @pl.when(last)
        def _():
            finalize(qi, acc)

        return m, mblk, acc

    m0 = jnp.full((1, 2 * tq), -jnp.inf, F32)
    a0 = jnp.zeros((V_DIM + ONES_ROWS, 2 * tq), F32)
    lax.fori_loop(0, nq * kg, body, (m0, scores(masked_q(0), 0, sa_ref), a0))


def _attention(qt, k, vt, lam_stack, g_stack, layer, tok_off, n_seq, seq_len, lambda_init, tq, tk):
    sb = tok_off // seq_len
    kern = functools.partial(_attn_kernel, seq_len=seq_len, tq=tq, tk=tk, lambda_init=lambda_init)
    return pl.pallas_call(
        kern,
        grid=(n_seq, ATTN_HEADS),
        in_specs=[pl.BlockSpec((None, 4, QK_DIM), lambda s, h: (layer, 0, 0)),
                  pl.BlockSpec((None, 1, V_DIM), lambda s, h: (layer, 0, 0)),
                  pl.BlockSpec((LANES, seq_len), lambda s, h: (h, sb + s)),
                  pl.BlockSpec((seq_len, LANES), lambda s, h: (sb + s, h)),
                  pl.BlockSpec((V_DIM + ONES_ROWS, seq_len), lambda s, h: (h, sb + s))],
        out_specs=pl.BlockSpec((seq_len, V_DIM), lambda s, h: (s, h)),
        out_shape=jax.ShapeDtypeStruct((n_seq * seq_len, ATTN_WIDTH), BF16),
        scratch_shapes=[pltpu.VMEM((tk, 2 * tq), F32), pltpu.VMEM((tk, 2 * tq), F32)],
        compiler_params=_cparams(("parallel", "parallel")),
        name="diff_attn",
    )(lam_stack, g_stack, qt, k, vt)


def _post_kernel(x_ref, ssd_ref, att_ref, wg_ref, bg_ref, wso_ref, wao_ref, wo_ref,
                 g_ref, b_ref, o_ref, ob_ref):
    x = x_ref[...]
    gates = _sigmoid(jnp.dot(x.astype(BF16), wg_ref[...], preferred_element_type=F32) + bg_ref[...])
    so = jnp.dot(ssd_ref[...], wso_ref[...], preferred_element_type=F32)
    ao = jnp.dot(att_ref[...], wao_ref[...], preferred_element_type=F32)
    merged = gates[:, :D_MODEL] * so + gates[:, D_MODEL:] * ao
    m = jnp.dot(merged.astype(BF16), wo_ref[...], preferred_element_type=F32)
    y = _layer_norm(ALPHA * x + m, g_ref[...], b_ref[...])
    o_ref[...] = y
    ob_ref[...] = y.astype(BF16)


def _resident(shape, layer):
    nd = len(shape)
    return pl.BlockSpec((None,) + tuple(shape), lambda i: (layer,) + (0,) * nd,
                        pipeline_mode=pl.Buffered(1))


def _post(x, ssd_y, att_o, wg, bg, wso, wao, wo, g, b, layer, tm):
    t = x.shape[0]
    tok = lambda w: pl.BlockSpec((tm, w), lambda i: (i, 0))
    return pl.pallas_call(
        _post_kernel,
        grid=(t // tm,),
        in_specs=[tok(D_MODEL), tok(D_INNER), tok(ATTN_WIDTH),
                  _resident((D_MODEL, 2 * D_MODEL), layer), _resident((1, 2 * D_MODEL), layer),
                  _resident((D_INNER, D_MODEL), layer), _resident((ATTN_WIDTH, D_MODEL), layer),
                  _resident((D_MODEL, D_MODEL), layer),
                  _resident((1, D_MODEL), layer), _resident((1, D_MODEL), layer)],
        out_specs=[tok(D_MODEL), tok(D_MODEL)],
        out_shape=[jax.ShapeDtypeStruct((t, D_MODEL), F32),
                   jax.ShapeDtypeStruct((t, D_MODEL), BF16)],
        compiler_params=_cparams(("parallel",)),
        name="merge_out_ln",
    )(x, ssd_y, att_o, wg, bg, wso, wao, wo, g, b)


FF_SPLIT = 2
FF_CHUNK = D_FF // FF_SPLIT


def _ffn_kernel(x_ref, wu_ref, wd_ref, g_ref, b_ref, o_ref, ob_ref):
    x = x_ref[...]
    xb = x.astype(BF16)
    f = None
    for c in range(FF_SPLIT):
        a = jnp.dot(xb, wu_ref[:, c * FF_CHUNK:(c + 1) * FF_CHUNK], preferred_element_type=F32)
        u = jnp.dot(xb, wu_ref[:, D_FF + c * FF_CHUNK:D_FF + (c + 1) * FF_CHUNK],
                    preferred_element_type=F32)
        act = (_silu(a) * u).astype(BF16)
        part = jnp.dot(act, wd_ref[c * FF_CHUNK:(c + 1) * FF_CHUNK, :], preferred_element_type=F32)
        f = part if f is None else f + part
    y = _layer_norm(ALPHA * x + f, g_ref[...], b_ref[...])
    o_ref[...] = y
    ob_ref[...] = y.astype(BF16)


def _ffn(x, wu, wd, g, b, layer, tm):
    t = x.shape[0]
    tok = pl.BlockSpec((tm, D_MODEL), lambda i: (i, 0))
    return pl.pallas_call(
        _ffn_kernel,
        grid=(t // tm,),
        in_specs=[tok, _resident((D_MODEL, 2 * D_FF), layer), _resident((D_FF, D_MODEL), layer),
                  _resident((1, D_MODEL), layer), _resident((1, D_MODEL), layer)],
        out_specs=[tok, tok],
        out_shape=[jax.ShapeDtypeStruct((t, D_MODEL), F32),
                   jax.ShapeDtypeStruct((t, D_MODEL), BF16)],
        compiler_params=_cparams(("parallel",)),
        name="swiglu_ln",
    )(x, wu, wd, g, b)


def _pad_lanes(p):
    flat = p.reshape(DEPTH, 1, DT_COLS).astype(F32)
    return jnp.pad(flat, ((0, 0), (0, 0), (0, LANES - DT_COLS)))


def kernel(x_prompt, x_sample, w_in, conv_w, conv_b, a_log, dt_bias, d_skip, ssd_norm_g, lam_qk,
           attn_norm_g, w_ssd_out, w_attn_out, w_gate, b_gate, w_out, ln1_g, ln1_b, w_ffn_up,
           w_ffn_down, ln2_g, ln2_b):
    x = jnp.concatenate([x_prompt.reshape(T_PROMPT, D_MODEL),
                         x_sample.reshape(N_SAMPLE * L_SAMPLE, D_MODEL)], axis=0)
    xb = x.astype(BF16)

    inv = ROPE_THETA ** (-jnp.arange(0, QK_DIM, 2, dtype=F32) / QK_DIM)
    ang = jnp.arange(L_PROMPT, dtype=F32)[:, None] * inv[None, :]
    cos, sin = jnp.cos(ang), jnp.sin(ang)
    cos_t = jnp.concatenate([cos, cos, cos, cos], axis=-1)
    sin_t = jnp.concatenate([-sin, sin, -sin, sin], axis=-1)

    w_zx = w_in[:, :, :ZX_COLS].astype(BF16)
    w_dt = jnp.pad(w_in[:, :, ZX_COLS:QKV_OFF], ((0, 0), (0, 0), (0, LANES - DT_COLS))).astype(BF16)
    w_qkv = w_in[:, :, QKV_OFF:].astype(BF16)
    w_g = w_gate.astype(BF16)
    w_so = w_ssd_out.astype(BF16)
    w_ao = w_attn_out.astype(BF16)
    w_o = w_out.astype(BF16)
    w_up = w_ffn_up.astype(BF16)
    w_dn = w_ffn_down.astype(BF16)
    conv_w_p = jnp.pad(conv_w, ((0, 0), (0, SUBLANES - CONV_WIDTH), (0, 0)))
    conv_b_p = conv_b.reshape(DEPTH, 1, CONV_DIM)
    alog_p = _pad_lanes(a_log)
    dtb_p = _pad_lanes(dt_bias)
    dskip_p = jnp.repeat(d_skip, SSD_HEAD_DIM, axis=-1).reshape(DEPTH, 1, D_INNER)
    ng_p = ssd_norm_g.reshape(DEPTH, 1, D_INNER)
    ag_p = attn_norm_g.reshape(DEPTH, 1, V_DIM)
    row = lambda p: p.reshape(DEPTH, 1, -1)

    for i in range(DEPTH):
        lambda_init = 0.8 - 0.6 * math.exp(-0.3 * i)
        zx = _matmul(xb, w_zx, i, 1024, 2048, F32)
        dt_raw = _matmul(xb, w_dt, i, 2048, LANES, F32)
        q_t = _head_proj(xb, w_qkv, i, 0, 1024, rope_tables=(cos_t, sin_t),
                         scale=QK_SCALE * LOG2_E, transpose=True)
        k_r = _head_proj(xb, w_qkv, i, 1, 1024, rope_tables=(cos_t, sin_t))
        v_t = _head_proj(xb, w_qkv, i, 2, 1024, transpose=True, ones_rows=ONES_ROWS)
        xs = _conv(zx, conv_w_p, conv_b_p, i, D_INNER, D_INNER, F32, 512, 1024)
        bc = _conv(zx, conv_w_p, conv_b_p, i, 2 * D_INNER, BC_COLS, BF16, 512, 1024)
        y_f = _ssd(xs, bc, dt_raw, alog_p, dtb_p, i, reverse=False)
        ssd_y = _ssd(xs, bc, dt_raw, alog_p, dtb_p, i, reverse=True,
                     extra=(y_f, zx, dskip_p, ng_p))
        att_p = _attention(q_t, k_r, v_t, lam_qk, ag_p, i, 0, 1, L_PROMPT, lambda_init, 256, 512)
        att_s = _attention(q_t, k_r, v_t, lam_qk, ag_p, i, T_PROMPT, N_SAMPLE, L_SAMPLE,
                           lambda_init, 256, 512)
        att_o = jnp.concatenate([att_p, att_s], axis=0)
        x, xb = _post(x, ssd_y, att_o, w_g, row(b_gate), w_so, w_ao, w_o, row(ln1_g), row(ln1_b), i, 512)
        x, xb = _ffn(x, w_up, w_dn, row(ln2_g), row(ln2_b), i, 512)

    y_prompt = x[:T_PROMPT].reshape(1, L_PROMPT, D_MODEL)
    y_sample = x[T_PROMPT:].reshape(N_SAMPLE, L_SAMPLE, D_MODEL)
    return (y_prompt, y_sample)
```

```python
import functools
import math

import jax
import jax.numpy as jnp
from jax import lax
from jax.experimental import pallas as pl
from jax.experimental.pallas import tpu as pltpu

F32 = jnp.float32
BF16 = jnp.bfloat16

D_MODEL = 1024
DEPTH = 4
L_PROMPT = 16384
N_SAMPLE = 8
L_SAMPLE = 4096
T_PROMPT = L_PROMPT
T_TOTAL = T_PROMPT + N_SAMPLE * L_SAMPLE

D_INNER = 2 * D_MODEL
SSD_HEAD_DIM = 64
SSD_HEADS = D_INNER // SSD_HEAD_DIM
SSD_GROUPS = 8
D_STATE = 128
CONV_WIDTH = 5
BC_COLS = 2 * SSD_GROUPS * D_STATE
CONV_DIM = D_INNER + BC_COLS
CHUNK = 128
ATTN_HEADS = 8
QK_DIM = 64
V_DIM = 128
ATTN_WIDTH = ATTN_HEADS * V_DIM
ROPE_THETA = 10000.0
D_FF = 2816
ZX_COLS = D_INNER + CONV_DIM
DT_COLS = 2 * SSD_HEADS
QKV_OFF = ZX_COLS + DT_COLS
ALPHA = (2 * DEPTH) ** 0.25
EPS = 1e-5
QK_SCALE = QK_DIM ** -0.5
LOG2_E = math.log2(math.e)

LANES = 128
SUBLANES = 8
VMEM_LIMIT = 56 * 1024 * 1024


def _cparams(sem):
    return pltpu.CompilerParams(dimension_semantics=sem, vmem_limit_bytes=VMEM_LIMIT)


def _seq_flags(t0, n):
    t1 = t0 + n
    first = (t0 == 0) | ((t0 >= T_PROMPT) & ((t0 - T_PROMPT) % L_SAMPLE == 0))
    last = (t1 == T_PROMPT) | ((t1 > T_PROMPT) & ((t1 - T_PROMPT) % L_SAMPLE == 0))
    return first, last


def _sigmoid(v):
    return 1.0 / (1.0 + jnp.exp(-v))


def _silu(v):
    return v * _sigmoid(v)


def _softplus(v):
    return jnp.maximum(v, 0.0) + jnp.log1p(jnp.exp(-jnp.abs(v)))


def _layer_norm(v, g, b):
    mu = jnp.mean(v, axis=-1, keepdims=True)
    d = v - mu
    var = jnp.mean(d * d, axis=-1, keepdims=True)
    return d * lax.rsqrt(var + EPS) * g + b


def _mm_kernel(x_ref, w_ref, o_ref):
    o_ref[...] = jnp.dot(x_ref[...], w_ref[...],
                         preferred_element_type=F32).astype(o_ref.dtype)


def _matmul(xb, w_stack, layer, tm, tn, out_dtype):
    t, k = xb.shape
    n = w_stack.shape[2]
    return pl.pallas_call(
        _mm_kernel,
        grid=(n // tn, t // tm),
        in_specs=[pl.BlockSpec((tm, k), lambda j, i: (i, 0)),
                  pl.BlockSpec((None, k, tn), lambda j, i: (layer, 0, j))],
        out_specs=pl.BlockSpec((tm, tn), lambda j, i: (i, j)),
        out_shape=jax.ShapeDtypeStruct((t, n), out_dtype),
        compiler_params=_cparams(("parallel", "parallel")),
        name="in_proj",
    )(xb, w_stack)


ONES_ROWS = 16


def _head_proj_kernel(*refs, rope, scale, transpose, ones_rows):
    if rope:
        x_ref, w_ref, cos_ref, sin_ref, o_ref = refs
    else:
        x_ref, w_ref, o_ref = refs
    acc = jnp.dot(x_ref[...], w_ref[...], preferred_element_type=F32)
    tm = acc.shape[0]
    if rope:
        c = cos_ref[...] * scale
        s = sin_ref[...] * scale
        lane = lax.broadcasted_iota(jnp.int32, (tm, LANES), 1)
        first_half = (lane % QK_DIM) < (QK_DIM // 2)
    hrows = LANES + ones_rows
    for b in range(ATTN_WIDTH // LANES):
        t = acc[:, b * LANES:(b + 1) * LANES]
        if rope:
            partner = jnp.where(first_half,
                                pltpu.roll(t, LANES - QK_DIM // 2, 1),
                                pltpu.roll(t, QK_DIM // 2, 1))
            t = t * c + partner * s
        if transpose:
            o_ref[b * hrows:b * hrows + LANES, :] = t.T.astype(o_ref.dtype)
            if ones_rows:
                o_ref[b * hrows + LANES:(b + 1) * hrows, :] = jnp.ones((ones_rows, tm), o_ref.dtype)
        else:
            o_ref[:, b * LANES:(b + 1) * LANES] = t.astype(o_ref.dtype)


def _head_proj(xb, w_stack, layer, part, tm, rope_tables=None, scale=1.0, transpose=False,
               ones_rows=0):
    t, k = xb.shape
    n1 = T_PROMPT // tm
    n2 = L_SAMPLE // tm
    pos_map = lambda i: (jnp.where(i < n1, i, (i - n1) % n2), 0)
    in_specs = [pl.BlockSpec((tm, k), lambda i: (i, 0)),
                pl.BlockSpec((None, k, ATTN_WIDTH), lambda i: (layer, 0, part))]
    args = [xb, w_stack]
    if rope_tables is not None:
        in_specs += [pl.BlockSpec((tm, LANES), pos_map), pl.BlockSpec((tm, LANES), pos_map)]
        args += list(rope_tables)
    if transpose:
        rows = ATTN_HEADS * (LANES + ones_rows)
        out_spec = pl.BlockSpec((rows, tm), lambda i: (0, i))
        out_shape = jax.ShapeDtypeStruct((rows, t), BF16)
    else:
        out_spec = pl.BlockSpec((tm, ATTN_WIDTH), lambda i: (i, 0))
        out_shape = jax.ShapeDtypeStruct((t, ATTN_WIDTH), BF16)
    kern = functools.partial(_head_proj_kernel, rope=rope_tables is not None, scale=scale,
                             transpose=transpose, ones_rows=ones_rows)
    return pl.pallas_call(
        kern,
        grid=(t // tm,),
        in_specs=in_specs,
        out_specs=out_spec,
        out_shape=out_shape,
        compiler_params=_cparams(("parallel",)),
        name="head_proj",
    )(*args)


def _conv_kernel(x_ref, prev_ref, next_ref, w_ref, b_ref, o_ref, buf_ref, *, tm):
    i = pl.program_id(0)
    first, last = _seq_flags(i * tm, tm)
    halo = SUBLANES
    buf_ref[0:halo, :] = jnp.where(first, 0.0, prev_ref[...])
    buf_ref[halo:halo + tm, :] = x_ref[...]
    buf_ref[halo + tm:2 * halo + tm, :] = jnp.where(last, 0.0, next_ref[...])
    rows = tm + 2 * halo
    xfull = buf_ref[...]
    acc = b_ref[...] + w_ref[CONV_WIDTH // 2:CONV_WIDTH // 2 + 1, :] * xfull[halo:halo + tm]
    for k in range(CONV_WIDTH):
        if k != CONV_WIDTH // 2:
            rolled = pltpu.roll(xfull, (CONV_WIDTH // 2 - k) % rows, 0)
            acc = acc + w_ref[k:k + 1, :] * rolled[halo:halo + tm]
    o_ref[...] = _silu(acc).astype(o_ref.dtype)


def _conv(zx, w_stack, b_stack, layer, col0, ncols, out_dtype, tm, tc):
    t = zx.shape[0]
    cb = col0 // tc
    pb = (col0 - D_INNER) // tc
    rb = tm // SUBLANES
    nrb = t // SUBLANES
    kern = functools.partial(_conv_kernel, tm=tm)
    return pl.pallas_call(
        kern,
        grid=(t // tm, ncols // tc),
        in_specs=[pl.BlockSpec((tm, tc), lambda i, j: (i, cb + j)),
                  pl.BlockSpec((SUBLANES, tc), lambda i, j: (jnp.maximum(i * rb - 1, 0), cb + j)),
                  pl.BlockSpec((SUBLANES, tc), lambda i, j: (jnp.minimum((i + 1) * rb, nrb - 1), cb + j)),
                  pl.BlockSpec((None, SUBLANES, tc), lambda i, j: (layer, 0, pb + j)),
                  pl.BlockSpec((None, 1, tc), lambda i, j: (layer, 0, pb + j))],
        out_specs=pl.BlockSpec((tm, tc), lambda i, j: (i, j)),
        out_shape=jax.ShapeDtypeStruct((t, ncols), out_dtype),
        scratch_shapes=[pltpu.VMEM((tm + 2 * SUBLANES, tc), F32)],
        compiler_params=_cparams(("parallel", "parallel")),
        name="conv_silu",
    )(zx, zx, zx, w_stack, b_stack)


def _ssd_kernel(*refs, reverse, final, n_chunks):
    if final:
        (xs_ref, bc_ref, dt_ref, alog_ref, dtb_ref, yf_ref, z_ref, dskip_ref, ng_ref,
         o_ref, st_ref) = refs
    else:
        xs_ref, bc_ref, dt_ref, alog_ref, dtb_ref, o_ref, st_ref = refs
    step = pl.program_id(0)
    c = (n_chunks - 1 - step) if reverse else step
    first, last = _seq_flags(c * CHUNK, CHUNK)

    @pl.when(last if reverse else first)
    def _():
        st_ref[...] = jnp.zeros_like(st_ref)

    q = CHUNK
    dt = _softplus(dt_ref[...] + dtb_ref[...])
    adt = dt * (-jnp.exp(alog_ref[...]) * LOG2_E)
    row = lax.broadcasted_iota(jnp.int32, (q, q), 0)
    col = lax.broadcasted_iota(jnp.int32, (q, q), 1)
    mask = (col >= row) if reverse else (col <= row)
    tri = jnp.where(mask, 1.0, 0.0).astype(BF16)
    adt_hi = adt.astype(BF16)
    rem = adt - adt_hi.astype(F32)
    adt_mid = rem.astype(BF16)
    adt_lo = (rem - adt_mid.astype(F32)).astype(BF16)
    ac = ((jnp.dot(tri, adt_lo, preferred_element_type=F32)
           + jnp.dot(tri, adt_mid, preferred_element_type=F32))
          + jnp.dot(tri, adt_hi, preferred_element_type=F32))
    ac_t = ac.T
    dt_t = dt.T
    lo = col < SSD_HEAD_DIM
    lo_row = lo[0:1, :]
    off = SSD_HEADS if reverse else 0
    end = 0 if reverse else q - 1
    pair = 2 * SSD_HEAD_DIM

    for g in range(SSD_GROUPS):
        bg = bc_ref[:, g * D_STATE:(g + 1) * D_STATE].astype(F32)
        cg = bc_ref[:, (SSD_GROUPS + g) * D_STATE:(SSD_GROUPS + g + 1) * D_STATE].astype(F32)
        bg_t = bg.T
        cb = jnp.dot(cg.astype(BF16), bg_t.astype(BF16), preferred_element_type=F32)
        y_parts = []
        for pr in range(2):
            h0 = g * 4 + pr * 2
            cols = slice(h0 * SSD_HEAD_DIM, h0 * SSD_HEAD_DIM + pair)
            xs_p = xs_ref[:, cols].astype(BF16)
            st_p = st_ref[:, cols]
            st_b = st_p.astype(BF16)
            zero = jnp.zeros_like(xs_p)
            x_lo = jnp.where(lo, xs_p, zero)
            x_hi = jnp.where(lo, zero, xs_p)
            s_lo = jnp.where(lo, st_b, zero)
            s_hi = jnp.where(lo, zero, st_b)
            m_parts, c_parts, b_parts, e_last = [], [], [], []
            for hh in range(2):
                ci = off + h0 + hh
                ac_col = jnp.broadcast_to(ac[:, ci:ci + 1], (q, q))
                ac_row = ac_t[ci:ci + 1, :]
                dt_row = dt_t[ci:ci + 1, :]
                decay = jnp.exp2(jnp.where(mask, ac_col - ac_row, -jnp.inf))
                m_parts.append((cb * decay * dt_row).astype(BF16))
                c_parts.append((cg * jnp.exp2(ac_col)).astype(BF16))
                a_last = ac_t[ci:ci + 1, end:end + 1]
                b_parts.append((bg_t * (jnp.exp2(a_last - ac_row) * dt_row)).astype(BF16))
                e_last.append(jnp.exp2(a_last))
            lhs_y = jnp.concatenate(m_parts + c_parts, axis=1)
            rhs_y = jnp.concatenate([x_lo, x_hi, s_lo, s_hi], axis=0)
            y_parts.append(jnp.dot(lhs_y, rhs_y, preferred_element_type=F32))
            lhs_s = jnp.concatenate(b_parts, axis=1)
            rhs_s = jnp.concatenate([x_lo, x_hi], axis=0)
            dec = jnp.where(lo_row, e_last[0], e_last[1])
            st_ref[:, cols] = st_p * dec + jnp.dot(lhs_s, rhs_s, preferred_element_type=F32)
        gcols = slice(g * 4 * SSD_HEAD_DIM, (g + 1) * 4 * SSD_HEAD_DIM)
        y_g = jnp.concatenate(y_parts, axis=1)
        if final:
            y_g = y_g + yf_ref[:, gcols] + xs_ref[:, gcols] * dskip_ref[:, gcols]
            y_g = y_g * _silu(z_ref[:, gcols])
            ms = jnp.mean(y_g * y_g, axis=-1, keepdims=True)
            o_ref[:, gcols] = (y_g * lax.rsqrt(ms + EPS) * ng_ref[:, gcols]).astype(o_ref.dtype)
        else:
            o_ref[:, gcols] = y_g


def _ssd(xs, bc, dt_raw, alog_p, dtb_p, layer, reverse, extra=None):
    t = xs.shape[0]
    nc = t // CHUNK
    final = extra is not None
    cm = (lambda s: (nc - 1 - s, 0)) if reverse else (lambda s: (s, 0))
    pm = lambda s: (layer, 0, 0)
    in_specs = [pl.BlockSpec((CHUNK, D_INNER), cm),
                pl.BlockSpec((CHUNK, BC_COLS), cm),
                pl.BlockSpec((CHUNK, LANES), cm),
                pl.BlockSpec((None, 1, LANES), pm),
                pl.BlockSpec((None, 1, LANES), pm)]
    args = [xs, bc, dt_raw, alog_p, dtb_p]
    if final:
        yf, zx, dskip_p, ng_p = extra
        in_specs += [pl.BlockSpec((CHUNK, D_INNER), cm),
                     pl.BlockSpec((CHUNK, D_INNER), cm),
                     pl.BlockSpec((None, 1, D_INNER), pm),
                     pl.BlockSpec((None, 1, D_INNER), pm)]
        args += [yf, zx, dskip_p, ng_p]
    kern = functools.partial(_ssd_kernel, reverse=reverse, final=final, n_chunks=nc)
    return pl.pallas_call(
        kern,
        grid=(nc,),
        in_specs=in_specs,
        out_specs=pl.BlockSpec((CHUNK, D_INNER), cm),
        out_shape=jax.ShapeDtypeStruct((t, D_INNER), BF16 if final else F32),
        scratch_shapes=[pltpu.VMEM((D_STATE, D_INNER), F32)],
        compiler_params=_cparams(("arbitrary",)),
        name="ssd_bwd" if reverse else "ssd_fwd",
    )(*args)


def _attn_kernel(lam_ref, g_ref, qt_ref, k_ref, vt_ref, o_ref, sa_ref, sb_ref, *,
                 seq_len, tq, tk, kb_per_iter, lambda_init):
    nq = seq_len // tq
    kg = seq_len // tk // kb_per_iter
    rowi = lax.broadcasted_iota(jnp.int32, (LANES, tq), 0)

    def masked_q(qi):
        qt = qt_ref[:, pl.ds(pl.multiple_of(qi * tq, tq), tq)]
        zero = jnp.zeros_like(qt)
        return jnp.concatenate([jnp.where(rowi < QK_DIM, qt, zero),
                                jnp.where(rowi < QK_DIM, zero, qt)], axis=1)

    def scores(qqt, j, s_ref):
        start = pl.multiple_of(j * tk, tk)
        st = jnp.dot(k_ref[pl.ds(start, tk), :], qqt, preferred_element_type=F32)
        s_ref[...] = st
        return jnp.max(st, axis=0, keepdims=True)

    def update(j, s_ref, m, mblk, acc):
        start = pl.multiple_of(j * tk, tk)
        vext = vt_ref[:, pl.ds(start, tk)]
        m_new = jnp.maximum(m, mblk)
        alpha = jnp.exp2(m - m_new)
        pt = jnp.exp2(s_ref[...] - m_new).astype(BF16)
        return m_new, alpha * acc + jnp.dot(vext, pt, preferred_element_type=F32)

    def finalize(qi, acc):
        o = acc[:V_DIM, :] * (1.0 / acc[V_DIM:V_DIM + 1, :])
        lq = lam_ref[...]
        lam = (jnp.exp(jnp.sum(lq[0:1, :] * lq[1:2, :], axis=-1, keepdims=True))
               - jnp.exp(jnp.sum(lq[2:3, :] * lq[3:4, :], axis=-1, keepdims=True)) + lambda_init)
        d = o[:, :tq] - lam * o[:, tq:]
        ms = jnp.mean(d * d, axis=0, keepdims=True)
        dn = (d * lax.rsqrt(ms + EPS)).T
        o_ref[pl.ds(pl.multiple_of(qi * tq, tq), tq), :] = (
            dn * g_ref[...] * (1.0 - lambda_init)).astype(o_ref.dtype)

    def body(u, carry):
        m, mblk, acc = carry
        qi = u // kg
        g = u % kg
        first = g == 0
        last = g == kg - 1
        qqt = masked_q(qi)
        qqt_next = masked_q(jnp.where(last, jnp.minimum(qi + 1, nq - 1), qi))
        j0 = g * kb_per_iter
        m = jnp.where(first, -jnp.inf, m)
        bufs = (sa_ref, sb_ref)
        for r in range(kb_per_iter):
            cur, nxt = bufs[r % 2], bufs[(r + 1) % 2]
            if r < kb_per_iter - 1:
                mblk_next = scores(qqt, j0 + r + 1, nxt)
            else:
                mblk_next = scores(qqt_next, jnp.where(last, 0, j0 + kb_per_iter), nxt)
            m, acc = update(j0 + r, cur, m, mblk, acc)
            mblk = mblk_next

        @pl.when(last)
        def _():
            finalize(qi, acc)

        return m, mblk, acc

    m0 = jnp.full((1, 2 * tq), -jnp.inf, F32)
    a0 = jnp.zeros((V_DIM + ONES_ROWS, 2 * tq), F32)
    lax.fori_loop(0, nq * kg, body, (m0, scores(masked_q(0), 0, sa_ref), a0))


def _attention(qt, k, vt, lam_stack, g_stack, layer, tok_off, n_seq, seq_len, lambda_init,
               tq, tk, kb_per_iter):
    sb = tok_off // seq_len
    kern = functools.partial(_attn_kernel, seq_len=seq_len, tq=tq, tk=tk,
                             kb_per_iter=kb_per_iter, lambda_init=lambda_init)
    return pl.pallas_call(
        kern,
        grid=(n_seq, ATTN_HEADS),
        in_specs=[pl.BlockSpec((None, 4, QK_DIM), lambda s, h: (layer, 0, 0)),
                  pl.BlockSpec((None, 1, V_DIM), lambda s, h: (layer, 0, 0)),
                  pl.BlockSpec((LANES, seq_len), lambda s, h: (h, sb + s)),
                  pl.BlockSpec((seq_len, LANES), lambda s, h: (sb + s, h)),
                  pl.BlockSpec((V_DIM + ONES_ROWS, seq_len), lambda s, h: (h, sb + s))],
        out_specs=pl.BlockSpec((seq_len, V_DIM), lambda s, h: (s, h)),
        out_shape=jax.ShapeDtypeStruct((n_seq * seq_len, ATTN_WIDTH), BF16),
        scratch_shapes=[pltpu.VMEM((tk, 2 * tq), F32), pltpu.VMEM((tk, 2 * tq), F32)],
        compiler_params=_cparams(("parallel", "parallel")),
        name="diff_attn",
    )(lam_stack, g_stack, qt, k, vt)


def _post_kernel(x_ref, ssd_ref, attp_ref, atts_ref, wg_ref, bg_ref, wso_ref, wao_ref, wo_ref,
                 g_ref, b_ref, o_ref, ob_ref, *, n_prompt_tiles):
    x = x_ref[...]
    gates = _sigmoid(jnp.dot(x.astype(BF16), wg_ref[...], preferred_element_type=F32) + bg_ref[...])
    so = jnp.dot(ssd_ref[...], wso_ref[...], preferred_element_type=F32)
    att = jnp.where(pl.program_id(0) < n_prompt_tiles, attp_ref[...], atts_ref[...])
    ao = jnp.dot(att, wao_ref[...], preferred_element_type=F32)
    merged = gates[:, :D_MODEL] * so + gates[:, D_MODEL:] * ao
    m = jnp.dot(merged.astype(BF16), wo_ref[...], preferred_element_type=F32)
    y = _layer_norm(ALPHA * x + m, g_ref[...], b_ref[...])
    o_ref[...] = y
    ob_ref[...] = y.astype(BF16)


def _resident(shape, layer):
    nd = len(shape)
    return pl.BlockSpec((None,) + tuple(shape), lambda i: (layer,) + (0,) * nd,
                        pipeline_mode=pl.Buffered(1))


def _post(x, ssd_y, att_p, att_s, wg, bg, wso, wao, wo, g, b, layer, tm):
    t = x.shape[0]
    n1 = att_p.shape[0] // tm
    n2 = att_s.shape[0] // tm
    tok = lambda w: pl.BlockSpec((tm, w), lambda i: (i, 0))
    return pl.pallas_call(
        functools.partial(_post_kernel, n_prompt_tiles=n1),
        grid=(t // tm,),
        in_specs=[tok(D_MODEL), tok(D_INNER),
                  pl.BlockSpec((tm, ATTN_WIDTH), lambda i: (jnp.minimum(i, n1 - 1), 0)),
                  pl.BlockSpec((tm, ATTN_WIDTH), lambda i: (jnp.clip(i - n1, 0, n2 - 1), 0)),
                  _resident((D_MODEL, 2 * D_MODEL), layer), _resident((1, 2 * D_MODEL), layer),
                  _resident((D_INNER, D_MODEL), layer), _resident((ATTN_WIDTH, D_MODEL), layer),
                  _resident((D_MODEL, D_MODEL), layer),
                  _resident((1, D_MODEL), layer), _resident((1, D_MODEL), layer)],
        out_specs=[tok(D_MODEL), tok(D_MODEL)],
        out_shape=[jax.ShapeDtypeStruct((t, D_MODEL), F32),
                   jax.ShapeDtypeStruct((t, D_MODEL), BF16)],
        compiler_params=_cparams(("parallel",)),
        name="merge_out_ln",
    )(x, ssd_y, att_p, att_s, wg, bg, wso, wao, wo, g, b)


FF_SPLIT = 2
FF_CHUNK = D_FF // FF_SPLIT


def _ffn_kernel(x_ref, wu_ref, wd_ref, g_ref, b_ref, o_ref, ob_ref):
    x = x_ref[...]
    xb = x.astype(BF16)
    f = None
    for c in range(FF_SPLIT):
        a = jnp.dot(xb, wu_ref[:, c * FF_CHUNK:(c + 1) * FF_CHUNK], preferred_element_type=F32)
        u = jnp.dot(xb, wu_ref[:, D_FF + c * FF_CHUNK:D_FF + (c + 1) * FF_CHUNK],
                    preferred_element_type=F32)
        act = (_silu(a) * u).astype(BF16)
        part = jnp.dot(act, wd_ref[c * FF_CHUNK:(c + 1) * FF_CHUNK, :], preferred_element_type=F32)
        f = part if f is None else f + part
    y = _layer_norm(ALPHA * x + f, g_ref[...], b_ref[...])
    o_ref[...] = y
    ob_ref[...] = y.astype(BF16)


def _ffn(x, wu, wd, g, b, layer, tm):
    t = x.shape[0]
    tok = pl.BlockSpec((tm, D_MODEL), lambda i: (i, 0))
    return pl.pallas_call(
        _ffn_kernel,
        grid=(t // tm,),
        in_specs=[tok, _resident((D_MODEL, 2 * D_FF), layer), _resident((D_FF, D_MODEL), layer),
                  _resident((1, D_MODEL), layer), _resident((1, D_MODEL), layer)],
        out_specs=[tok, tok],
        out_shape=[jax.ShapeDtypeStruct((t, D_MODEL), F32),
                   jax.ShapeDtypeStruct((t, D_MODEL), BF16)],
        compiler_params=_cparams(("parallel",)),
        name="swiglu_ln",
    )(x, wu, wd, g, b)


def _pad_lanes(p):
    flat = p.reshape(DEPTH, 1, DT_COLS).astype(F32)
    return jnp.pad(flat, ((0, 0), (0, 0), (0, LANES - DT_COLS)))


def kernel(x_prompt, x_sample, w_in, conv_w, conv_b, a_log, dt_bias, d_skip, ssd_norm_g, lam_qk,
           attn_norm_g, w_ssd_out, w_attn_out, w_gate, b_gate, w_out, ln1_g, ln1_b, w_ffn_up,
           w_ffn_down, ln2_g, ln2_b):
    x = jnp.concatenate([x_prompt.reshape(T_PROMPT, D_MODEL),
                         x_sample.reshape(N_SAMPLE * L_SAMPLE, D_MODEL)], axis=0)
    xb = x.astype(BF16)

    inv = ROPE_THETA ** (-jnp.arange(0, QK_DIM, 2, dtype=F32) / QK_DIM)
    ang = jnp.arange(L_PROMPT, dtype=F32)[:, None] * inv[None, :]
    cos, sin = jnp.cos(ang), jnp.sin(ang)
    cos_t = jnp.concatenate([cos, cos, cos, cos], axis=-1)
    sin_t = jnp.concatenate([-sin, sin, -sin, sin], axis=-1)

    w_zx = w_in[:, :, :ZX_COLS].astype(BF16)
    w_dt = jnp.pad(w_in[:, :, ZX_COLS:QKV_OFF], ((0, 0), (0, 0), (0, LANES - DT_COLS))).astype(BF16)
    w_qkv = w_in[:, :, QKV_OFF:].astype(BF16)
    w_g = w_gate.astype(BF16)
    w_so = w_ssd_out.astype(BF16)
    w_ao = w_attn_out.astype(BF16)
    w_o = w_out.astype(BF16)
    w_up = w_ffn_up.astype(BF16)
    w_dn = w_ffn_down.astype(BF16)
    conv_w_p = jnp.pad(conv_w, ((0, 0), (0, SUBLANES - CONV_WIDTH), (0, 0)))
    conv_b_p = conv_b.reshape(DEPTH, 1, CONV_DIM)
    alog_p = _pad_lanes(a_log)
    dtb_p = _pad_lanes(dt_bias)
    dskip_p = jnp.repeat(d_skip, SSD_HEAD_DIM, axis=-1).reshape(DEPTH, 1, D_INNER)
    ng_p = ssd_norm_g.reshape(DEPTH, 1, D_INNER)
    ag_p = attn_norm_g.reshape(DEPTH, 1, V_DIM)
    row = lambda p: p.reshape(DEPTH, 1, -1)

    for i in range(DEPTH):
        lambda_init = 0.8 - 0.6 * math.exp(-0.3 * i)
        zx = _matmul(xb, w_zx, i, 1024, 2048, F32)
        dt_raw = _matmul(xb, w_dt, i, 2048, LANES, F32)
        q_t = _head_proj(xb, w_qkv, i, 0, 1024, rope_tables=(cos_t, sin_t),
                         scale=QK_SCALE * LOG2_E, transpose=True)
        k_r = _head_proj(xb, w_qkv, i, 1, 1024, rope_tables=(cos_t, sin_t))
        v_t = _head_proj(xb, w_qkv, i, 2, 1024, transpose=True, ones_rows=ONES_ROWS)
        xs = _conv(zx, conv_w_p, conv_b_p, i, D_INNER, D_INNER, F32, 512, 1024)
        bc = _conv(zx, conv_w_p, conv_b_p, i, 2 * D_INNER, BC_COLS, BF16, 512, 1024)
        y_f = _ssd(xs, bc, dt_raw, alog_p, dtb_p, i, reverse=False)
        ssd_y = _ssd(xs, bc, dt_raw, alog_p, dtb_p, i, reverse=True,
                     extra=(y_f, zx, dskip_p, ng_p))
        att_p = _attention(q_t, k_r, v_t, lam_qk, ag_p, i, 0, 1, L_PROMPT, lambda_init,
                           256, 512, 8)
        att_s = _attention(q_t, k_r, v_t, lam_qk, ag_p, i, T_PROMPT, N_SAMPLE, L_SAMPLE,
                           lambda_init, 256, 512, 8)
        x, xb = _post(x, ssd_y, att_p, att_s, w_g, row(b_gate), w_so, w_ao, w_o,
                      row(ln1_g), row(ln1_b), i, 512)
        x, xb = _ffn(x, w_up, w_dn, row(ln2_g), row(ln2_b), i, 512)

    y_prompt = x[:T_PROMPT].reshape(1, L_PROMPT, D_MODEL)
    y_sample = x[T_PROMPT:].reshape(N_SAMPLE, L_SAMPLE, D_MODEL)
    return (y_prompt, y_sample)
```

```python
import functools
import math

import jax
import jax.numpy as jnp
from jax import lax
from jax.experimental import pallas as pl
from jax.experimental.pallas import tpu as pltpu

F32 = jnp.float32
BF16 = jnp.bfloat16

D_MODEL = 1024
DEPTH = 4
L_PROMPT = 16384
N_SAMPLE = 8
L_SAMPLE = 4096
T_PROMPT = L_PROMPT
T_TOTAL = T_PROMPT + N_SAMPLE * L_SAMPLE

D_INNER = 2 * D_MODEL
SSD_HEAD_DIM = 64
SSD_HEADS = D_INNER // SSD_HEAD_DIM
SSD_GROUPS = 8
D_STATE = 128
CONV_WIDTH = 5
BC_COLS = 2 * SSD_GROUPS * D_STATE
CONV_DIM = D_INNER + BC_COLS
CHUNK = 128
ATTN_HEADS = 8
QK_DIM = 64
V_DIM = 128
ATTN_WIDTH = ATTN_HEADS * V_DIM
ROPE_THETA = 10000.0
D_FF = 2816
ZX_COLS = D_INNER + CONV_DIM
DT_COLS = 2 * SSD_HEADS
QKV_OFF = ZX_COLS + DT_COLS
ALPHA = (2 * DEPTH) ** 0.25
EPS = 1e-5
QK_SCALE = QK_DIM ** -0.5
LOG2_E = math.log2(math.e)

LANES = 128
SUBLANES = 8
VMEM_LIMIT = 56 * 1024 * 1024


def _cparams(sem):
    return pltpu.CompilerParams(dimension_semantics=sem, vmem_limit_bytes=VMEM_LIMIT)


def _seq_flags(t0, n):
    t1 = t0 + n
    first = (t0 == 0) | ((t0 >= T_PROMPT) & ((t0 - T_PROMPT) % L_SAMPLE == 0))
    last = (t1 == T_PROMPT) | ((t1 > T_PROMPT) & ((t1 - T_PROMPT) % L_SAMPLE == 0))
    return first, last


def _sigmoid(v):
    return 1.0 / (1.0 + jnp.exp(-v))


def _silu(v):
    return v * _sigmoid(v)


def _softplus(v):
    return jnp.maximum(v, 0.0) + jnp.log1p(jnp.exp(-jnp.abs(v)))


def _layer_norm(v, g, b):
    mu = jnp.mean(v, axis=-1, keepdims=True)
    d = v - mu
    var = jnp.mean(d * d, axis=-1, keepdims=True)
    return d * lax.rsqrt(var + EPS) * g + b


def _mm_kernel(x_ref, w_ref, o_ref):
    o_ref[...] = jnp.dot(x_ref[...], w_ref[...],
                         preferred_element_type=F32).astype(o_ref.dtype)


def _matmul(xb, w_stack, layer, tm, tn, out_dtype):
    t, k = xb.shape
    n = w_stack.shape[2]
    return pl.pallas_call(
        _mm_kernel,
        grid=(n // tn, t // tm),
        in_specs=[pl.BlockSpec((tm, k), lambda j, i: (i, 0)),
                  pl.BlockSpec((None, k, tn), lambda j, i: (layer, 0, j))],
        out_specs=pl.BlockSpec((tm, tn), lambda j, i: (i, j)),
        out_shape=jax.ShapeDtypeStruct((t, n), out_dtype),
        compiler_params=_cparams(("parallel", "parallel")),
        name="in_proj",
    )(xb, w_stack)


ONES_ROWS = 16


def _head_proj_kernel(*refs, rope, scale, transpose, ones_rows):
    if rope:
        x_ref, w_ref, cos_ref, sin_ref, o_ref = refs
    else:
        x_ref, w_ref, o_ref = refs
    acc = jnp.dot(x_ref[...], w_ref[...], preferred_element_type=F32)
    tm = acc.shape[0]
    if rope:
        c = cos_ref[...] * scale
        s = sin_ref[...] * scale
        lane = lax.broadcasted_iota(jnp.int32, (tm, LANES), 1)
        first_half = (lane % QK_DIM) < (QK_DIM // 2)
    hrows = LANES + ones_rows
    for b in range(ATTN_WIDTH // LANES):
        t = acc[:, b * LANES:(b + 1) * LANES]
        if rope:
            partner = jnp.where(first_half,
                                pltpu.roll(t, LANES - QK_DIM // 2, 1),
                                pltpu.roll(t, QK_DIM // 2, 1))
            t = t * c + partner * s
        if transpose:
            o_ref[b * hrows:b * hrows + LANES, :] = t.T.astype(o_ref.dtype)
            if ones_rows:
                o_ref[b * hrows + LANES:(b + 1) * hrows, :] = jnp.ones((ones_rows, tm), o_ref.dtype)
        else:
            o_ref[:, b * LANES:(b + 1) * LANES] = t.astype(o_ref.dtype)


def _head_proj(xb, w_stack, layer, part, tm, rope_tables=None, scale=1.0, transpose=False,
               ones_rows=0):
    t, k = xb.shape
    n1 = T_PROMPT // tm
    n2 = L_SAMPLE // tm
    pos_map = lambda i: (jnp.where(i < n1, i, (i - n1) % n2), 0)
    in_specs = [pl.BlockSpec((tm, k), lambda i: (i, 0)),
                pl.BlockSpec((None, k, ATTN_WIDTH), lambda i: (layer, 0, part))]
    args = [xb, w_stack]
    if rope_tables is not None:
        in_specs += [pl.BlockSpec((tm, LANES), pos_map), pl.BlockSpec((tm, LANES), pos_map)]
        args += list(rope_tables)
    if transpose:
        rows = ATTN_HEADS * (LANES + ones_rows)
        out_spec = pl.BlockSpec((rows, tm), lambda i: (0, i))
        out_shape = jax.ShapeDtypeStruct((rows, t), BF16)
    else:
        out_spec = pl.BlockSpec((tm, ATTN_WIDTH), lambda i: (i, 0))
        out_shape = jax.ShapeDtypeStruct((t, ATTN_WIDTH), BF16)
    kern = functools.partial(_head_proj_kernel, rope=rope_tables is not None, scale=scale,
                             transpose=transpose, ones_rows=ones_rows)
    return pl.pallas_call(
        kern,
        grid=(t // tm,),
        in_specs=in_specs,
        out_specs=out_spec,
        out_shape=out_shape,
        compiler_params=_cparams(("parallel",)),
        name="head_proj",
    )(*args)


def _conv_kernel(x_ref, prev_ref, next_ref, w_ref, b_ref, o_ref, buf_ref, *, tm):
    i = pl.program_id(0)
    first, last = _seq_flags(i * tm, tm)
    halo = SUBLANES
    buf_ref[0:halo, :] = jnp.where(first, 0.0, prev_ref[...])
    buf_ref[halo:halo + tm, :] = x_ref[...]
    buf_ref[halo + tm:2 * halo + tm, :] = jnp.where(last, 0.0, next_ref[...])
    rows = tm + 2 * halo
    xfull = buf_ref[...]
    acc = b_ref[...] + w_ref[CONV_WIDTH // 2:CONV_WIDTH // 2 + 1, :] * xfull[halo:halo + tm]
    for k in range(CONV_WIDTH):
        if k != CONV_WIDTH // 2:
            rolled = pltpu.roll(xfull, (CONV_WIDTH // 2 - k) % rows, 0)
            acc = acc + w_ref[k:k + 1, :] * rolled[halo:halo + tm]
    o_ref[...] = _silu(acc).astype(o_ref.dtype)


def _conv(zx, w_stack, b_stack, layer, col0, ncols, out_dtype, tm, tc):
    t = zx.shape[0]
    cb = col0 // tc
    pb = (col0 - D_INNER) // tc
    rb = tm // SUBLANES
    nrb = t // SUBLANES
    kern = functools.partial(_conv_kernel, tm=tm)
    return pl.pallas_call(
        kern,
        grid=(t // tm, ncols // tc),
        in_specs=[pl.BlockSpec((tm, tc), lambda i, j: (i, cb + j)),
                  pl.BlockSpec((SUBLANES, tc), lambda i, j: (jnp.maximum(i * rb - 1, 0), cb + j)),
                  pl.BlockSpec((SUBLANES, tc), lambda i, j: (jnp.minimum((i + 1) * rb, nrb - 1), cb + j)),
                  pl.BlockSpec((None, SUBLANES, tc), lambda i, j: (layer, 0, pb + j)),
                  pl.BlockSpec((None, 1, tc), lambda i, j: (layer, 0, pb + j))],
        out_specs=pl.BlockSpec((tm, tc), lambda i, j: (i, j)),
        out_shape=jax.ShapeDtypeStruct((t, ncols), out_dtype),
        scratch_shapes=[pltpu.VMEM((tm + 2 * SUBLANES, tc), F32)],
        compiler_params=_cparams(("parallel", "parallel")),
        name="conv_silu",
    )(zx, zx, zx, w_stack, b_stack)


def _ssd_prep_kernel(x_ref, w_ref, alog_ref, dtb_ref, ac_ref, act_ref, dtt_ref):
    q = CHUNK
    dt = _softplus(jnp.dot(x_ref[...], w_ref[...], preferred_element_type=F32) + dtb_ref[...])
    adt = dt * (-jnp.exp(alog_ref[...]) * LOG2_E)
    adt_hi = adt.astype(BF16)
    rem = adt - adt_hi.astype(F32)
    adt_mid = rem.astype(BF16)
    adt_lo = (rem - adt_mid.astype(F32)).astype(BF16)
    row = lax.broadcasted_iota(jnp.int32, (q, q), 0)
    col = lax.broadcasted_iota(jnp.int32, (q, q), 1)
    tri_f = jnp.where(col <= row, 1.0, 0.0).astype(BF16)
    tri_b = jnp.where(col >= row, 1.0, 0.0).astype(BF16)
    fwd_col = col < SSD_HEADS
    for c in range(x_ref.shape[0] // q):
        rows = slice(c * q, (c + 1) * q)
        cums = []
        for tri in (tri_f, tri_b):
            cums.append((jnp.dot(tri, adt_lo[rows], preferred_element_type=F32)
                         + jnp.dot(tri, adt_mid[rows], preferred_element_type=F32))
                        + jnp.dot(tri, adt_hi[rows], preferred_element_type=F32))
        ac = jnp.where(fwd_col, cums[0], cums[1])
        ac_ref[rows, :] = ac
        act_ref[rows, :] = ac.T
        dtt_ref[rows, :] = dt[rows].T


def _ssd_prep(xb, w_dt, alog_p, dtb_p, layer, tm):
    t, k = xb.shape
    tok = pl.BlockSpec((tm, LANES), lambda i: (i, 0))
    pm = lambda i: (layer, 0, 0)
    shape = jax.ShapeDtypeStruct((t, LANES), F32)
    return pl.pallas_call(
        _ssd_prep_kernel,
        grid=(t // tm,),
        in_specs=[pl.BlockSpec((tm, k), lambda i: (i, 0)),
                  pl.BlockSpec((None, k, LANES), pm),
                  pl.BlockSpec((None, 1, LANES), pm),
                  pl.BlockSpec((None, 1, LANES), pm)],
        out_specs=[tok, tok, tok],
        out_shape=[shape, shape, shape],
        compiler_params=_cparams(("parallel",)),
        name="ssd_prep",
    )(xb, w_dt, alog_p, dtb_p)


def _ssd_kernel(*refs, reverse, final, n_chunks):
    if final:
        (xs_ref, bc_ref, ac_ref, act_ref, dtt_ref, yf_ref, z_ref, dskip_ref, ng_ref,
         o_ref, st_ref) = refs
    else:
        xs_ref, bc_ref, ac_ref, act_ref, dtt_ref, o_ref, st_ref = refs
    step = pl.program_id(0)
    c = (n_chunks - 1 - step) if reverse else step
    first, last = _seq_flags(c * CHUNK, CHUNK)

    @pl.when(last if reverse else first)
    def _():
        st_ref[...] = jnp.zeros_like(st_ref)

    q = CHUNK
    row = lax.broadcasted_iota(jnp.int32, (q, q), 0)
    col = lax.broadcasted_iota(jnp.int32, (q, q), 1)
    mask = (col >= row) if reverse else (col <= row)
    ac = ac_ref[...]
    ac_t = act_ref[...]
    dt_t = dtt_ref[...]
    lo = col < SSD_HEAD_DIM
    lo_row = lo[0:1, :]
    off = SSD_HEADS if reverse else 0
    end = 0 if reverse else q - 1
    pair = 2 * SSD_HEAD_DIM

    for g in range(SSD_GROUPS):
        bg = bc_ref[:, g * D_STATE:(g + 1) * D_STATE].astype(F32)
        cg = bc_ref[:, (SSD_GROUPS + g) * D_STATE:(SSD_GROUPS + g + 1) * D_STATE].astype(F32)
        bg_t = bg.T
        cb = jnp.dot(cg.astype(BF16), bg_t.astype(BF16), preferred_element_type=F32)
        y_parts = []
        for pr in range(2):
            h0 = g * 4 + pr * 2
            cols = slice(h0 * SSD_HEAD_DIM, h0 * SSD_HEAD_DIM + pair)
            xs_p = xs_ref[:, cols].astype(BF16)
            st_p = st_ref[:, cols]
            st_b = st_p.astype(BF16)
            zero = jnp.zeros_like(xs_p)
            x_lo = jnp.where(lo, xs_p, zero)
            x_hi = jnp.where(lo, zero, xs_p)
            s_lo = jnp.where(lo, st_b, zero)
            s_hi = jnp.where(lo, zero, st_b)
            m_parts, c_parts, b_parts, e_last = [], [], [], []
            for hh in range(2):
                ci = off + h0 + hh
                ac_col = jnp.broadcast_to(ac[:, ci:ci + 1], (q, q))
                ac_row = ac_t[ci:ci + 1, :]
                dt_row = dt_t[ci:ci + 1, :]
                decay = jnp.exp2(jnp.where(mask, ac_col - ac_row, -jnp.inf))
                m_parts.append((cb * decay * dt_row).astype(BF16))
                c_parts.append((cg * jnp.exp2(ac_col)).astype(BF16))
                a_last = ac_t[ci:ci + 1, end:end + 1]
                b_parts.append((bg_t * (jnp.exp2(a_last - ac_row) * dt_row)).astype(BF16))
                e_last.append(jnp.exp2(a_last))
            lhs_y = jnp.concatenate(m_parts + c_parts, axis=1)
            rhs_y = jnp.concatenate([x_lo, x_hi, s_lo, s_hi], axis=0)
            y_parts.append(jnp.dot(lhs_y, rhs_y, preferred_element_type=F32))
            lhs_s = jnp.concatenate(b_parts, axis=1)
            rhs_s = jnp.concatenate([x_lo, x_hi], axis=0)
            dec = jnp.where(lo_row, e_last[0], e_last[1])
            st_ref[:, cols] = st_p * dec + jnp.dot(lhs_s, rhs_s, preferred_element_type=F32)
        gcols = slice(g * 4 * SSD_HEAD_DIM, (g + 1) * 4 * SSD_HEAD_DIM)
        y_g = jnp.concatenate(y_parts, axis=1)
        if final:
            y_g = y_g + yf_ref[:, gcols] + xs_ref[:, gcols] * dskip_ref[:, gcols]
            y_g = y_g * _silu(z_ref[:, gcols])
            ms = jnp.mean(y_g * y_g, axis=-1, keepdims=True)
            o_ref[:, gcols] = (y_g * lax.rsqrt(ms + EPS) * ng_ref[:, gcols]).astype(o_ref.dtype)
        else:
            o_ref[:, gcols] = y_g


def _ssd(xs, bc, prep, layer, reverse, extra=None):
    t = xs.shape[0]
    nc = t // CHUNK
    final = extra is not None
    cm = (lambda s: (nc - 1 - s, 0)) if reverse else (lambda s: (s, 0))
    pm = lambda s: (layer, 0, 0)
    in_specs = [pl.BlockSpec((CHUNK, D_INNER), cm),
                pl.BlockSpec((CHUNK, BC_COLS), cm),
                pl.BlockSpec((CHUNK, LANES), cm),
                pl.BlockSpec((CHUNK, LANES), cm),
                pl.BlockSpec((CHUNK, LANES), cm)]
    args = [xs, bc, *prep]
    if final:
        yf, zx, dskip_p, ng_p = extra
        in_specs += [pl.BlockSpec((CHUNK, D_INNER), cm),
                     pl.BlockSpec((CHUNK, D_INNER), cm),
                     pl.BlockSpec((None, 1, D_INNER), pm),
                     pl.BlockSpec((None, 1, D_INNER), pm)]
        args += [yf, zx, dskip_p, ng_p]
    kern = functools.partial(_ssd_kernel, reverse=reverse, final=final, n_chunks=nc)
    return pl.pallas_call(
        kern,
        grid=(nc,),
        in_specs=in_specs,
        out_specs=pl.BlockSpec((CHUNK, D_INNER), cm),
        out_shape=jax.ShapeDtypeStruct((t, D_INNER), BF16 if final else F32),
        scratch_shapes=[pltpu.VMEM((D_STATE, D_INNER), F32)],
        compiler_params=_cparams(("arbitrary",)),
        name="ssd_bwd" if reverse else "ssd_fwd",
    )(*args)


def _attn_kernel(lam_ref, g_ref, qt_ref, k_ref, vt_ref, o_ref, sa_ref, sb_ref, *,
                 seq_len, tq, tk, kb_per_iter, lambda_init):
    nq = seq_len // tq
    kg = seq_len // tk // kb_per_iter
    rowi = lax.broadcasted_iota(jnp.int32, (LANES, tq), 0)

    def masked_q(qi):
        qt = qt_ref[:, pl.ds(pl.multiple_of(qi * tq, tq), tq)]
        zero = jnp.zeros_like(qt)
        return jnp.concatenate([jnp.where(rowi < QK_DIM, qt, zero),
                                jnp.where(rowi < QK_DIM, zero, qt)], axis=1)

    def scores(qqt, j, s_ref):
        start = pl.multiple_of(j * tk, tk)
        st = jnp.dot(k_ref[pl.ds(start, tk), :], qqt, preferred_element_type=F32)
        s_ref[...] = st
        return jnp.max(st, axis=0, keepdims=True)

    def update(j, s_ref, m, mblk, acc):
        start = pl.multiple_of(j * tk, tk)
        vext = vt_ref[:, pl.ds(start, tk)]
        m_new = jnp.maximum(m, mblk)
        alpha = jnp.exp2(m - m_new)
        pt = jnp.exp2(s_ref[...] - m_new).astype(BF16)
        return m_new, alpha * acc + jnp.dot(vext, pt, preferred_element_type=F32)

    def finalize(qi, acc):
        o = acc[:V_DIM, :] * (1.0 / acc[V_DIM:V_DIM + 1, :])
        lq = lam_ref[...]
        lam = (jnp.exp(jnp.sum(lq[0:1, :] * lq[1:2, :], axis=-1, keepdims=True))
               - jnp.exp(jnp.sum(lq[2:3, :] * lq[3:4, :], axis=-1, keepdims=True)) + lambda_init)
        d = o[:, :tq] - lam * o[:, tq:]
        ms = jnp.mean(d * d, axis=0, keepdims=True)
        dn = (d * lax.rsqrt(ms + EPS)).T
        o_ref[pl.ds(pl.multiple_of(qi * tq, tq), tq), :] = (
            dn * g_ref[...] * (1.0 - lambda_init)).astype(o_ref.dtype)

    def body(u, carry):
        m, mblk, acc = carry
        qi = u // kg
        g = u % kg
        first = g == 0
        last = g == kg - 1
        qqt = masked_q(qi)
        qqt_next = masked_q(jnp.where(last, jnp.minimum(qi + 1, nq - 1), qi))
        j0 = g * kb_per_iter
        m = jnp.where(first, -jnp.inf, m)
        bufs = (sa_ref, sb_ref)
        for r in range(kb_per_iter):
            cur, nxt = bufs[r % 2], bufs[(r + 1) % 2]
            if r < kb_per_iter - 1:
                mblk_next = scores(qqt, j0 + r + 1, nxt)
            else:
                mblk_next = scores(qqt_next, jnp.where(last, 0, j0 + kb_per_iter), nxt)
            m, acc = update(j0 + r, cur, m, mblk, acc)
            mblk = mblk_next

        @pl.when(last)
        def _():
            finalize(qi, acc)

        return m, mblk, acc

    m0 = jnp.full((1, 2 * tq), -jnp.inf, F32)
    a0 = jnp.zeros((V_DIM + ONES_ROWS, 2 * tq), F32)
    lax.fori_loop(0, nq * kg, body, (m0, scores(masked_q(0), 0, sa_ref), a0))


def _attention(qt, k, vt, lam_stack, g_stack, layer, tok_off, n_seq, seq_len, lambda_init,
               tq, tk, kb_per_iter):
    sb = tok_off // seq_len
    kern = functools.partial(_attn_kernel, seq_len=seq_len, tq=tq, tk=tk,
                             kb_per_iter=kb_per_iter, lambda_init=lambda_init)
    return pl.pallas_call(
        kern,
        grid=(n_seq, ATTN_HEADS),
        in_specs=[pl.BlockSpec((None, 4, QK_DIM), lambda s, h: (layer, 0, 0)),
                  pl.BlockSpec((None, 1, V_DIM), lambda s, h: (layer, 0, 0)),
                  pl.BlockSpec((LANES, seq_len), lambda s, h: (h, sb + s)),
                  pl.BlockSpec((seq_len, LANES), lambda s, h: (sb + s, h)),
                  pl.BlockSpec((V_DIM + ONES_ROWS, seq_len), lambda s, h: (h, sb + s))],
        out_specs=pl.BlockSpec((seq_len, V_DIM), lambda s, h: (s, h)),
        out_shape=jax.ShapeDtypeStruct((n_seq * seq_len, ATTN_WIDTH), BF16),
        scratch_shapes=[pltpu.VMEM((tk, 2 * tq), F32), pltpu.VMEM((tk, 2 * tq), F32)],
        compiler_params=_cparams(("parallel", "parallel")),
        name="diff_attn",
    )(lam_stack, g_stack, qt, k, vt)


def _post_kernel(x_ref, ssd_ref, attp_ref, atts_ref, wg_ref, bg_ref, wso_ref, wao_ref, wo_ref,
                 g_ref, b_ref, o_ref, ob_ref, *, n_prompt_tiles):
    x = x_ref[...]
    gates = _sigmoid(jnp.dot(x.astype(BF16), wg_ref[...], preferred_element_type=F32) + bg_ref[...])
    so = jnp.dot(ssd_ref[...], wso_ref[...], preferred_element_type=F32)
    att = jnp.where(pl.program_id(0) < n_prompt_tiles, attp_ref[...], atts_ref[...])
    ao = jnp.dot(att, wao_ref[...], preferred_element_type=F32)
    merged = gates[:, :D_MODEL] * so + gates[:, D_MODEL:] * ao
    m = jnp.dot(merged.astype(BF16), wo_ref[...], preferred_element_type=F32)
    y = _layer_norm(ALPHA * x + m, g_ref[...], b_ref[...])
    o_ref[...] = y
    ob_ref[...] = y.astype(BF16)


def _resident(shape, layer):
    nd = len(shape)
    return pl.BlockSpec((None,) + tuple(shape), lambda i: (layer,) + (0,) * nd,
                        pipeline_mode=pl.Buffered(1))


def _post(x, ssd_y, att_p, att_s, wg, bg, wso, wao, wo, g, b, layer, tm):
    t = x.shape[0]
    n1 = att_p.shape[0] // tm
    n2 = att_s.shape[0] // tm
    tok = lambda w: pl.BlockSpec((tm, w), lambda i: (i, 0))
    return pl.pallas_call(
        functools.partial(_post_kernel, n_prompt_tiles=n1),
        grid=(t // tm,),
        in_specs=[tok(D_MODEL), tok(D_INNER),
                  pl.BlockSpec((tm, ATTN_WIDTH), lambda i: (jnp.minimum(i, n1 - 1), 0)),
                  pl.BlockSpec((tm, ATTN_WIDTH), lambda i: (jnp.clip(i - n1, 0, n2 - 1), 0)),
                  _resident((D_MODEL, 2 * D_MODEL), layer), _resident((1, 2 * D_MODEL), layer),
                  _resident((D_INNER, D_MODEL), layer), _resident((ATTN_WIDTH, D_MODEL), layer),
                  _resident((D_MODEL, D_MODEL), layer),
                  _resident((1, D_MODEL), layer), _resident((1, D_MODEL), layer)],
        out_specs=[tok(D_MODEL), tok(D_MODEL)],
        out_shape=[jax.ShapeDtypeStruct((t, D_MODEL), F32),
                   jax.ShapeDtypeStruct((t, D_MODEL), BF16)],
        compiler_params=_cparams(("parallel",)),
        name="merge_out_ln",
    )(x, ssd_y, att_p, att_s, wg, bg, wso, wao, wo, g, b)


FF_SPLIT = 2
FF_CHUNK = D_FF // FF_SPLIT


def _ffn_kernel(x_ref, wu_ref, wd_ref, g_ref, b_ref, o_ref, ob_ref):
    x = x_ref[...]
    xb = x.astype(BF16)
    f = None
    for c in range(FF_SPLIT):
        a = jnp.dot(xb, wu_ref[:, c * FF_CHUNK:(c + 1) * FF_CHUNK], preferred_element_type=F32)
        u = jnp.dot(xb, wu_ref[:, D_FF + c * FF_CHUNK:D_FF + (c + 1) * FF_CHUNK],
                    preferred_element_type=F32)
        act = (_silu(a) * u).astype(BF16)
        part = jnp.dot(act, wd_ref[c * FF_CHUNK:(c + 1) * FF_CHUNK, :], preferred_element_type=F32)
        f = part if f is None else f + part
    y = _layer_norm(ALPHA * x + f, g_ref[...], b_ref[...])
    o_ref[...] = y
    ob_ref[...] = y.astype(BF16)


def _ffn(x, wu, wd, g, b, layer, tm):
    t = x.shape[0]
    tok = pl.BlockSpec((tm, D_MODEL), lambda i: (i, 0))
    return pl.pallas_call(
        _ffn_kernel,
        grid=(t // tm,),
        in_specs=[tok, _resident((D_MODEL, 2 * D_FF), layer), _resident((D_FF, D_MODEL), layer),
                  _resident((1, D_MODEL), layer), _resident((1, D_MODEL), layer)],
        out_specs=[tok, tok],
        out_shape=[jax.ShapeDtypeStruct((t, D_MODEL), F32),
                   jax.ShapeDtypeStruct((t, D_MODEL), BF16)],
        compiler_params=_cparams(("parallel",)),
        name="swiglu_ln",
    )(x, wu, wd, g, b)


def _pad_lanes(p):
    flat = p.reshape(DEPTH, 1, DT_COLS).astype(F32)
    return jnp.pad(flat, ((0, 0), (0, 0), (0, LANES - DT_COLS)))


def kernel(x_prompt, x_sample, w_in, conv_w, conv_b, a_log, dt_bias, d_skip, ssd_norm_g, lam_qk,
           attn_norm_g, w_ssd_out, w_attn_out, w_gate, b_gate, w_out, ln1_g, ln1_b, w_ffn_up,
           w_ffn_down, ln2_g, ln2_b):
    x = jnp.concatenate([x_prompt.reshape(T_PROMPT, D_MODEL),
                         x_sample.reshape(N_SAMPLE * L_SAMPLE, D_MODEL)], axis=0)
    xb = x.astype(BF16)

    inv = ROPE_THETA ** (-jnp.arange(0, QK_DIM, 2, dtype=F32) / QK_DIM)
    ang = jnp.arange(L_PROMPT, dtype=F32)[:, None] * inv[None, :]
    cos, sin = jnp.cos(ang), jnp.sin(ang)
    cos_t = jnp.concatenate([cos, cos, cos, cos], axis=-1)
    sin_t = jnp.concatenate([-sin, sin, -sin, sin], axis=-1)

    w_zx = w_in[:, :, :ZX_COLS].astype(BF16)
    w_dt = jnp.pad(w_in[:, :, ZX_COLS:QKV_OFF], ((0, 0), (0, 0), (0, LANES - DT_COLS))).astype(BF16)
    w_qkv = w_in[:, :, QKV_OFF:].astype(BF16)
    w_g = w_gate.astype(BF16)
    w_so = w_ssd_out.astype(BF16)
    w_ao = w_attn_out.astype(BF16)
    w_o = w_out.astype(BF16)
    w_up = w_ffn_up.astype(BF16)
    w_dn = w_ffn_down.astype(BF16)
    conv_w_p = jnp.pad(conv_w, ((0, 0), (0, SUBLANES - CONV_WIDTH), (0, 0)))
    conv_b_p = conv_b.reshape(DEPTH, 1, CONV_DIM)
    alog_p = _pad_lanes(a_log)
    dtb_p = _pad_lanes(dt_bias)
    dskip_p = jnp.repeat(d_skip, SSD_HEAD_DIM, axis=-1).reshape(DEPTH, 1, D_INNER)
    ng_p = ssd_norm_g.reshape(DEPTH, 1, D_INNER)
    ag_p = attn_norm_g.reshape(DEPTH, 1, V_DIM)
    row = lambda p: p.reshape(DEPTH, 1, -1)

    for i in range(DEPTH):
        lambda_init = 0.8 - 0.6 * math.exp(-0.3 * i)
        zx = _matmul(xb, w_zx, i, 1024, 2048, F32)
        prep = _ssd_prep(xb, w_dt, alog_p, dtb_p, i, 1024)
        q_t = _head_proj(xb, w_qkv, i, 0, 1024, rope_tables=(cos_t, sin_t),
                         scale=QK_SCALE * LOG2_E, transpose=True)
        k_r = _head_proj(xb, w_qkv, i, 1, 1024, rope_tables=(cos_t, sin_t))
        v_t = _head_proj(xb, w_qkv, i, 2, 1024, transpose=True, ones_rows=ONES_ROWS)
        xs = _conv(zx, conv_w_p, conv_b_p, i, D_INNER, D_INNER, F32, 512, 1024)
        bc = _conv(zx, conv_w_p, conv_b_p, i, 2 * D_INNER, BC_COLS, BF16, 512, 1024)
        y_f = _ssd(xs, bc, prep, i, reverse=False)
        ssd_y = _ssd(xs, bc, prep, i, reverse=True, extra=(y_f, zx, dskip_p, ng_p))
        att_p = _attention(q_t, k_r, v_t, lam_qk, ag_p, i, 0, 1, L_PROMPT, lambda_init,
                           256, 512, 16)
        att_s = _attention(q_t, k_r, v_t, lam_qk, ag_p, i, T_PROMPT, N_SAMPLE, L_SAMPLE,
                           lambda_init, 256, 512, 8)
        x, xb = _post(x, ssd_y, att_p, att_s, w_g, row(b_gate), w_so, w_ao, w_o,
                      row(ln1_g), row(ln1_b), i, 512)
        x, xb = _ffn(x, w_up, w_dn, row(ln2_g), row(ln2_b), i, 512)

    y_prompt = x[:T_PROMPT].reshape(1, L_PROMPT, D_MODEL)
    y_sample = x[T_PROMPT:].reshape(N_SAMPLE, L_SAMPLE, D_MODEL)
    return (y_prompt, y_sample)
```

```python
import functools
import math

import jax
import jax.numpy as jnp
from jax import lax
from jax.experimental import pallas as pl
from jax.experimental.pallas import tpu as pltpu

F32 = jnp.float32
BF16 = jnp.bfloat16

D_MODEL = 1024
DEPTH = 4
L_PROMPT = 16384
N_SAMPLE = 8
L_SAMPLE = 4096
T_PROMPT = L_PROMPT
T_TOTAL = T_PROMPT + N_SAMPLE * L_SAMPLE

D_INNER = 2 * D_MODEL
SSD_HEAD_DIM = 64
SSD_HEADS = D_INNER // SSD_HEAD_DIM
SSD_GROUPS = 8
D_STATE = 128
CONV_WIDTH = 5
BC_COLS = 2 * SSD_GROUPS * D_STATE
CONV_DIM = D_INNER + BC_COLS
CHUNK = 128
ATTN_HEADS = 8
QK_DIM = 64
V_DIM = 128
ATTN_WIDTH = ATTN_HEADS * V_DIM
ROPE_THETA = 10000.0
D_FF = 2816
ZX_COLS = D_INNER + CONV_DIM
DT_COLS = 2 * SSD_HEADS
QKV_OFF = ZX_COLS + DT_COLS
ALPHA = (2 * DEPTH) ** 0.25
EPS = 1e-5
QK_SCALE = QK_DIM ** -0.5
LOG2_E = math.log2(math.e)

LANES = 128
SUBLANES = 8
VMEM_LIMIT = 56 * 1024 * 1024


def _cparams(sem):
    return pltpu.CompilerParams(dimension_semantics=sem, vmem_limit_bytes=VMEM_LIMIT)


def _seq_flags(t0, n):
    t1 = t0 + n
    first = (t0 == 0) | ((t0 >= T_PROMPT) & ((t0 - T_PROMPT) % L_SAMPLE == 0))
    last = (t1 == T_PROMPT) | ((t1 > T_PROMPT) & ((t1 - T_PROMPT) % L_SAMPLE == 0))
    return first, last


def _sigmoid(v):
    return 1.0 / (1.0 + jnp.exp(-v))


def _silu(v):
    return v * _sigmoid(v)


def _softplus(v):
    return jnp.maximum(v, 0.0) + jnp.log1p(jnp.exp(-jnp.abs(v)))


def _layer_norm(v, g, b):
    mu = jnp.mean(v, axis=-1, keepdims=True)
    d = v - mu
    var = jnp.mean(d * d, axis=-1, keepdims=True)
    return d * lax.rsqrt(var + EPS) * g + b


def _mm_kernel(x_ref, w_ref, o_ref):
    o_ref[...] = jnp.dot(x_ref[...], w_ref[...],
                         preferred_element_type=F32).astype(o_ref.dtype)


def _matmul(xb, w_stack, layer, tm, tn, out_dtype):
    t, k = xb.shape
    n = w_stack.shape[2]
    return pl.pallas_call(
        _mm_kernel,
        grid=(n // tn, t // tm),
        in_specs=[pl.BlockSpec((tm, k), lambda j, i: (i, 0)),
                  pl.BlockSpec((None, k, tn), lambda j, i: (layer, 0, j))],
        out_specs=pl.BlockSpec((tm, tn), lambda j, i: (i, j)),
        out_shape=jax.ShapeDtypeStruct((t, n), out_dtype),
        compiler_params=_cparams(("parallel", "parallel")),
        name="in_proj",
    )(xb, w_stack)


ONES_ROWS = 16


def _head_proj_kernel(*refs, rope, scale, transpose, ones_rows):
    if rope:
        x_ref, w_ref, cos_ref, sin_ref, o_ref = refs
    else:
        x_ref, w_ref, o_ref = refs
    acc = jnp.dot(x_ref[...], w_ref[...], preferred_element_type=F32)
    tm = acc.shape[0]
    if rope:
        c = cos_ref[...] * scale
        s = sin_ref[...] * scale
        lane = lax.broadcasted_iota(jnp.int32, (tm, LANES), 1)
        first_half = (lane % QK_DIM) < (QK_DIM // 2)
    hrows = LANES + ones_rows
    for b in range(ATTN_WIDTH // LANES):
        t = acc[:, b * LANES:(b + 1) * LANES]
        if rope:
            partner = jnp.where(first_half,
                                pltpu.roll(t, LANES - QK_DIM // 2, 1),
                                pltpu.roll(t, QK_DIM // 2, 1))
            t = t * c + partner * s
        if transpose:
            o_ref[b * hrows:b * hrows + LANES, :] = t.T.astype(o_ref.dtype)
            if ones_rows:
                o_ref[b * hrows + LANES:(b + 1) * hrows, :] = jnp.ones((ones_rows, tm), o_ref.dtype)
        else:
            o_ref[:, b * LANES:(b + 1) * LANES] = t.astype(o_ref.dtype)


def _head_proj(xb, w_stack, layer, part, tm, rope_tables=None, scale=1.0, transpose=False,
               ones_rows=0):
    t, k = xb.shape
    n1 = T_PROMPT // tm
    n2 = L_SAMPLE // tm
    pos_map = lambda i: (jnp.where(i < n1, i, (i - n1) % n2), 0)
    in_specs = [pl.BlockSpec((tm, k), lambda i: (i, 0)),
                pl.BlockSpec((None, k, ATTN_WIDTH), lambda i: (layer, 0, part))]
    args = [xb, w_stack]
    if rope_tables is not None:
        in_specs += [pl.BlockSpec((tm, LANES), pos_map), pl.BlockSpec((tm, LANES), pos_map)]
        args += list(rope_tables)
    if transpose:
        rows = ATTN_HEADS * (LANES + ones_rows)
        out_spec = pl.BlockSpec((rows, tm), lambda i: (0, i))
        out_shape = jax.ShapeDtypeStruct((rows, t), BF16)
    else:
        out_spec = pl.BlockSpec((tm, ATTN_WIDTH), lambda i: (i, 0))
        out_shape = jax.ShapeDtypeStruct((t, ATTN_WIDTH), BF16)
    kern = functools.partial(_head_proj_kernel, rope=rope_tables is not None, scale=scale,
                             transpose=transpose, ones_rows=ones_rows)
    return pl.pallas_call(
        kern,
        grid=(t // tm,),
        in_specs=in_specs,
        out_specs=out_spec,
        out_shape=out_shape,
        compiler_params=_cparams(("parallel",)),
        name="head_proj",
    )(*args)


def _conv_kernel(x_ref, prev_ref, next_ref, w_ref, b_ref, o_ref, buf_ref, *, tm):
    i = pl.program_id(0)
    first, last = _seq_flags(i * tm, tm)
    halo = SUBLANES
    buf_ref[0:halo, :] = jnp.where(first, 0.0, prev_ref[...])
    buf_ref[halo:halo + tm, :] = x_ref[...]
    buf_ref[halo + tm:2 * halo + tm, :] = jnp.where(last, 0.0, next_ref[...])
    rows = tm + 2 * halo
    xfull = buf_ref[...]
    acc = b_ref[...] + w_ref[CONV_WIDTH // 2:CONV_WIDTH // 2 + 1, :] * xfull[halo:halo + tm]
    for k in range(CONV_WIDTH):
        if k != CONV_WIDTH // 2:
            rolled = pltpu.roll(xfull, (CONV_WIDTH // 2 - k) % rows, 0)
            acc = acc + w_ref[k:k + 1, :] * rolled[halo:halo + tm]
    o_ref[...] = _silu(acc).astype(o_ref.dtype)


def _conv(zx, w_stack, b_stack, layer, col0, ncols, out_dtype, tm, tc):
    t = zx.shape[0]
    cb = col0 // tc
    pb = (col0 - D_INNER) // tc
    rb = tm // SUBLANES
    nrb = t // SUBLANES
    kern = functools.partial(_conv_kernel, tm=tm)
    return pl.pallas_call(
        kern,
        grid=(t // tm, ncols // tc),
        in_specs=[pl.BlockSpec((tm, tc), lambda i, j: (i, cb + j)),
                  pl.BlockSpec((SUBLANES, tc), lambda i, j: (jnp.maximum(i * rb - 1, 0), cb + j)),
                  pl.BlockSpec((SUBLANES, tc), lambda i, j: (jnp.minimum((i + 1) * rb, nrb - 1), cb + j)),
                  pl.BlockSpec((None, SUBLANES, tc), lambda i, j: (layer, 0, pb + j)),
                  pl.BlockSpec((None, 1, tc), lambda i, j: (layer, 0, pb + j))],
        out_specs=pl.BlockSpec((tm, tc), lambda i, j: (i, j)),
        out_shape=jax.ShapeDtypeStruct((t, ncols), out_dtype),
        scratch_shapes=[pltpu.VMEM((tm + 2 * SUBLANES, tc), F32)],
        compiler_params=_cparams(("parallel", "parallel")),
        name="conv_silu",
    )(zx, zx, zx, w_stack, b_stack)


def _ssd_prep_kernel(x_ref, w_ref, alog_ref, dtb_ref, ac_ref, act_ref, dtt_ref):
    q = CHUNK
    dt = _softplus(jnp.dot(x_ref[...], w_ref[...], preferred_element_type=F32) + dtb_ref[...])
    adt = dt * (-jnp.exp(alog_ref[...]) * LOG2_E)
    adt_hi = adt.astype(BF16)
    rem = adt - adt_hi.astype(F32)
    adt_mid = rem.astype(BF16)
    adt_lo = (rem - adt_mid.astype(F32)).astype(BF16)
    row = lax.broadcasted_iota(jnp.int32, (q, q), 0)
    col = lax.broadcasted_iota(jnp.int32, (q, q), 1)
    tri_f = jnp.where(col <= row, 1.0, 0.0).astype(BF16)
    tri_b = jnp.where(col >= row, 1.0, 0.0).astype(BF16)
    fwd_col = col < SSD_HEADS
    for c in range(x_ref.shape[0] // q):
        rows = slice(c * q, (c + 1) * q)
        cums = []
        for tri in (tri_f, tri_b):
            cums.append((jnp.dot(tri, adt_lo[rows], preferred_element_type=F32)
                         + jnp.dot(tri, adt_mid[rows], preferred_element_type=F32))
                        + jnp.dot(tri, adt_hi[rows], preferred_element_type=F32))
        ac = jnp.where(fwd_col, cums[0], cums[1])
        ac_ref[rows, :] = ac
        act_ref[rows, :] = ac.T
        dtt_ref[rows, :] = dt[rows].T


def _ssd_prep(xb, w_dt, alog_p, dtb_p, layer, tm):
    t, k = xb.shape
    tok = pl.BlockSpec((tm, LANES), lambda i: (i, 0))
    pm = lambda i: (layer, 0, 0)
    shape = jax.ShapeDtypeStruct((t, LANES), F32)
    return pl.pallas_call(
        _ssd_prep_kernel,
        grid=(t // tm,),
        in_specs=[pl.BlockSpec((tm, k), lambda i: (i, 0)),
                  pl.BlockSpec((None, k, LANES), pm),
                  pl.BlockSpec((None, 1, LANES), pm),
                  pl.BlockSpec((None, 1, LANES), pm)],
        out_specs=[tok, tok, tok],
        out_shape=[shape, shape, shape],
        compiler_params=_cparams(("parallel",)),
        name="ssd_prep",
    )(xb, w_dt, alog_p, dtb_p)


def _ssd_kernel(*refs, reverse, final, n_chunks):
    if final:
        (xs_ref, bc_ref, ac_ref, act_ref, dtt_ref, yf_ref, z_ref, dskip_ref, ng_ref,
         o_ref, st_ref) = refs
    else:
        xs_ref, bc_ref, ac_ref, act_ref, dtt_ref, o_ref, st_ref = refs
    step = pl.program_id(0)
    c = (n_chunks - 1 - step) if reverse else step
    first, last = _seq_flags(c * CHUNK, CHUNK)

    @pl.when(last if reverse else first)
    def _():
        st_ref[...] = jnp.zeros_like(st_ref)

    q = CHUNK
    row = lax.broadcasted_iota(jnp.int32, (q, q), 0)
    col = lax.broadcasted_iota(jnp.int32, (q, q), 1)
    mask = (col >= row) if reverse else (col <= row)
    ac = ac_ref[...]
    ac_t = act_ref[...]
    dt_t = dtt_ref[...]
    lo = col < SSD_HEAD_DIM
    lo_row = lo[0:1, :]
    off = SSD_HEADS if reverse else 0
    end = 0 if reverse else q - 1
    pair = 2 * SSD_HEAD_DIM

    for g in range(SSD_GROUPS):
        bg = bc_ref[:, g * D_STATE:(g + 1) * D_STATE].astype(F32)
        cg = bc_ref[:, (SSD_GROUPS + g) * D_STATE:(SSD_GROUPS + g + 1) * D_STATE].astype(F32)
        bg_t = bg.T
        cb = jnp.dot(cg.astype(BF16), bg_t.astype(BF16), preferred_element_type=F32)
        y_parts = []
        for pr in range(2):
            h0 = g * 4 + pr * 2
            cols = slice(h0 * SSD_HEAD_DIM, h0 * SSD_HEAD_DIM + pair)
            xs_p = xs_ref[:, cols].astype(BF16)
            st_p = st_ref[:, cols]
            st_b = st_p.astype(BF16)
            zero = jnp.zeros_like(xs_p)
            x_lo = jnp.where(lo, xs_p, zero)
            x_hi = jnp.where(lo, zero, xs_p)
            s_lo = jnp.where(lo, st_b, zero)
            s_hi = jnp.where(lo, zero, st_b)
            m_parts, c_parts, b_parts, e_last = [], [], [], []
            for hh in range(2):
                ci = off + h0 + hh
                ac_col = jnp.broadcast_to(ac[:, ci:ci + 1], (q, q))
                ac_row = ac_t[ci:ci + 1, :]
                dt_row = dt_t[ci:ci + 1, :]
                decay = jnp.exp2(jnp.where(mask, ac_col - ac_row, -jnp.inf))
                m_parts.append((cb * decay * dt_row).astype(BF16))
                c_parts.append((cg * jnp.exp2(ac_col)).astype(BF16))
                a_last = ac_t[ci:ci + 1, end:end + 1]
                b_parts.append((bg_t * (jnp.exp2(a_last - ac_row) * dt_row)).astype(BF16))
                e_last.append(jnp.exp2(a_last))
            lhs_y = jnp.concatenate(m_parts + c_parts, axis=1)
            rhs_y = jnp.concatenate([x_lo, x_hi, s_lo, s_hi], axis=0)
            y_parts.append(jnp.dot(lhs_y, rhs_y, preferred_element_type=F32))
            lhs_s = jnp.concatenate(b_parts, axis=1)
            rhs_s = jnp.concatenate([x_lo, x_hi], axis=0)
            dec = jnp.where(lo_row, e_last[0], e_last[1])
            st_ref[:, cols] = st_p * dec + jnp.dot(lhs_s, rhs_s, preferred_element_type=F32)
        gcols = slice(g * 4 * SSD_HEAD_DIM, (g + 1) * 4 * SSD_HEAD_DIM)
        y_g = jnp.concatenate(y_parts, axis=1)
        if final:
            y_g = y_g + yf_ref[:, gcols] + xs_ref[:, gcols] * dskip_ref[:, gcols]
            y_g = y_g * _silu(z_ref[:, gcols])
            ms = jnp.mean(y_g * y_g, axis=-1, keepdims=True)
            o_ref[:, gcols] = (y_g * lax.rsqrt(ms + EPS) * ng_ref[:, gcols]).astype(o_ref.dtype)
        else:
            o_ref[:, gcols] = y_g


def _ssd(xs, bc, prep, layer, reverse, extra=None):
    t = xs.shape[0]
    nc = t // CHUNK
    final = extra is not None
    cm = (lambda s: (nc - 1 - s, 0)) if reverse else (lambda s: (s, 0))
    pm = lambda s: (layer, 0, 0)
    in_specs = [pl.BlockSpec((CHUNK, D_INNER), cm),
                pl.BlockSpec((CHUNK, BC_COLS), cm),
                pl.BlockSpec((CHUNK, LANES), cm),
                pl.BlockSpec((CHUNK, LANES), cm),
                pl.BlockSpec((CHUNK, LANES), cm)]
    args = [xs, bc, *prep]
    if final:
        yf, zx, dskip_p, ng_p = extra
        in_specs += [pl.BlockSpec((CHUNK, D_INNER), cm),
                     pl.BlockSpec((CHUNK, D_INNER), cm),
                     pl.BlockSpec((None, 1, D_INNER), pm),
                     pl.BlockSpec((None, 1, D_INNER), pm)]
        args += [yf, zx, dskip_p, ng_p]
    kern = functools.partial(_ssd_kernel, reverse=reverse, final=final, n_chunks=nc)
    return pl.pallas_call(
        kern,
        grid=(nc,),
        in_specs=in_specs,
        out_specs=pl.BlockSpec((CHUNK, D_INNER), cm),
        out_shape=jax.ShapeDtypeStruct((t, D_INNER), BF16 if final else F32),
        scratch_shapes=[pltpu.VMEM((D_STATE, D_INNER), F32)],
        compiler_params=_cparams(("arbitrary",)),
        name="ssd_bwd" if reverse else "ssd_fwd",
    )(*args)


def _attn_kernel(lam_ref, g_ref, qt_ref, k_ref, vt_ref, o_ref, sa_ref, sb_ref, *,
                 seq_len, tq, tk, kb_per_iter, lambda_init):
    nq = seq_len // tq
    kg = seq_len // tk // kb_per_iter
    rowi = lax.broadcasted_iota(jnp.int32, (LANES, tq), 0)

    def masked_q(qi):
        qt = qt_ref[:, pl.ds(pl.multiple_of(qi * tq, tq), tq)]
        zero = jnp.zeros_like(qt)
        return jnp.concatenate([jnp.where(rowi < QK_DIM, qt, zero),
                                jnp.where(rowi < QK_DIM, zero, qt)], axis=1)

    def scores(qqt, j, s_ref):
        start = pl.multiple_of(j * tk, tk)
        st = jnp.dot(k_ref[pl.ds(start, tk), :], qqt, preferred_element_type=F32)
        s_ref[...] = st
        return jnp.max(st, axis=0, keepdims=True)

    def update(j, s_ref, m, mblk, acc):
        start = pl.multiple_of(j * tk, tk)
        vext = vt_ref[:, pl.ds(start, tk)]
        m_new = jnp.maximum(m, mblk)
        alpha = jnp.exp2(m - m_new)
        pt = jnp.exp2(s_ref[...] - m_new).astype(BF16)
        return m_new, alpha * acc + jnp.dot(vext, pt, preferred_element_type=F32)

    def finalize(qi, acc):
        o = acc[:V_DIM, :] * (1.0 / acc[V_DIM:V_DIM + 1, :])
        lq = lam_ref[...]
        lam = (jnp.exp(jnp.sum(lq[0:1, :] * lq[1:2, :], axis=-1, keepdims=True))
               - jnp.exp(jnp.sum(lq[2:3, :] * lq[3:4, :], axis=-1, keepdims=True)) + lambda_init)
        d = o[:, :tq] - lam * o[:, tq:]
        ms = jnp.mean(d * d, axis=0, keepdims=True)
        dn = (d * lax.rsqrt(ms + EPS)).T
        o_ref[pl.ds(pl.multiple_of(qi * tq, tq), tq), :] = (
            dn * g_ref[...] * (1.0 - lambda_init)).astype(o_ref.dtype)

    def body(u, carry):
        m, mblk, acc = carry
        qi = u // kg
        g = u % kg
        first = g == 0
        last = g == kg - 1
        finalize(jnp.maximum(u - 1, 0) // kg, acc)
        qqt = masked_q(qi)
        qqt_next = masked_q(jnp.where(last, jnp.minimum(qi + 1, nq - 1), qi))
        j0 = g * kb_per_iter
        m = jnp.where(first, -jnp.inf, m)
        bufs = (sa_ref, sb_ref)
        for r in range(kb_per_iter):
            cur, nxt = bufs[r % 2], bufs[(r + 1) % 2]
            if r < kb_per_iter - 1:
                mblk_next = scores(qqt, j0 + r + 1, nxt)
            else:
                mblk_next = scores(qqt_next, jnp.where(last, 0, j0 + kb_per_iter), nxt)
            m, acc = update(j0 + r, cur, m, mblk, acc)
            mblk = mblk_next
        return m, mblk, acc

    m0 = jnp.full((1, 2 * tq), -jnp.inf, F32)
    a0 = jnp.ones((V_DIM + ONES_ROWS, 2 * tq), F32)
    _, _, acc = lax.fori_loop(0, nq * kg, body, (m0, scores(masked_q(0), 0, sa_ref), a0))
    finalize(nq - 1, acc)


def _attention(qt, k, vt, lam_stack, g_stack, layer, tok_off, n_seq, seq_len, lambda_init,
               tq, tk, kb_per_iter):
    sb = tok_off // seq_len
    kern = functools.partial(_attn_kernel, seq_len=seq_len, tq=tq, tk=tk,
                             kb_per_iter=kb_per_iter, lambda_init=lambda_init)
    return pl.pallas_call(
        kern,
        grid=(n_seq, ATTN_HEADS),
        in_specs=[pl.BlockSpec((None, 4, QK_DIM), lambda s, h: (layer, 0, 0)),
                  pl.BlockSpec((None, 1, V_DIM), lambda s, h: (layer, 0, 0)),
                  pl.BlockSpec((LANES, seq_len), lambda s, h: (h, sb + s)),
                  pl.BlockSpec((seq_len, LANES), lambda s, h: (sb + s, h)),
                  pl.BlockSpec((V_DIM + ONES_ROWS, seq_len), lambda s, h: (h, sb + s))],
        out_specs=pl.BlockSpec((seq_len, V_DIM), lambda s, h: (s, h)),
        out_shape=jax.ShapeDtypeStruct((n_seq * seq_len, ATTN_WIDTH), BF16),
        scratch_shapes=[pltpu.VMEM((tk, 2 * tq), F32), pltpu.VMEM((tk, 2 * tq), F32)],
        compiler_params=_cparams(("parallel", "parallel")),
        name="diff_attn",
    )(lam_stack, g_stack, qt, k, vt)


def _post_kernel(x_ref, ssd_ref, attp_ref, atts_ref, wg_ref, bg_ref, wso_ref, wao_ref, wo_ref,
                 g_ref, b_ref, o_ref, ob_ref, *, n_prompt_tiles):
    x = x_ref[...]
    gates = _sigmoid(jnp.dot(x.astype(BF16), wg_ref[...], preferred_element_type=F32) + bg_ref[...])
    so = jnp.dot(ssd_ref[...], wso_ref[...], preferred_element_type=F32)
    att = jnp.where(pl.program_id(0) < n_prompt_tiles, attp_ref[...], atts_ref[...])
    ao = jnp.dot(att, wao_ref[...], preferred_element_type=F32)
    merged = gates[:, :D_MODEL] * so + gates[:, D_MODEL:] * ao
    m = jnp.dot(merged.astype(BF16), wo_ref[...], preferred_element_type=F32)
    y = _layer_norm(ALPHA * x + m, g_ref[...], b_ref[...])
    o_ref[...] = y
    ob_ref[...] = y.astype(BF16)


def _resident(shape, layer):
    nd = len(shape)
    return pl.BlockSpec((None,) + tuple(shape), lambda i: (layer,) + (0,) * nd,
                        pipeline_mode=pl.Buffered(1))


def _post(x, ssd_y, att_p, att_s, wg, bg, wso, wao, wo, g, b, layer, tm):
    t = x.shape[0]
    n1 = att_p.shape[0] // tm
    n2 = att_s.shape[0] // tm
    tok = lambda w: pl.BlockSpec((tm, w), lambda i: (i, 0))
    return pl.pallas_call(
        functools.partial(_post_kernel, n_prompt_tiles=n1),
        grid=(t // tm,),
        in_specs=[tok(D_MODEL), tok(D_INNER),
                  pl.BlockSpec((tm, ATTN_WIDTH), lambda i: (jnp.minimum(i, n1 - 1), 0)),
                  pl.BlockSpec((tm, ATTN_WIDTH), lambda i: (jnp.clip(i - n1, 0, n2 - 1), 0)),
                  _resident((D_MODEL, 2 * D_MODEL), layer), _resident((1, 2 * D_MODEL), layer),
                  _resident((D_INNER, D_MODEL), layer), _resident((ATTN_WIDTH, D_MODEL), layer),
                  _resident((D_MODEL, D_MODEL), layer),
                  _resident((1, D_MODEL), layer), _resident((1, D_MODEL), layer)],
        out_specs=[tok(D_MODEL), tok(D_MODEL)],
        out_shape=[jax.ShapeDtypeStruct((t, D_MODEL), F32),
                   jax.ShapeDtypeStruct((t, D_MODEL), BF16)],
        compiler_params=_cparams(("parallel",)),
        name="merge_out_ln",
    )(x, ssd_y, att_p, att_s, wg, bg, wso, wao, wo, g, b)


FF_SPLIT = 2
FF_CHUNK = D_FF // FF_SPLIT


def _ffn_kernel(x_ref, wu_ref, wd_ref, g_ref, b_ref, o_ref, ob_ref):
    x = x_ref[...]
    xb = x.astype(BF16)
    f = None
    for c in range(FF_SPLIT):
        a = jnp.dot(xb, wu_ref[:, c * FF_CHUNK:(c + 1) * FF_CHUNK], preferred_element_type=F32)
        u = jnp.dot(xb, wu_ref[:, D_FF + c * FF_CHUNK:D_FF + (c + 1) * FF_CHUNK],
                    preferred_element_type=F32)
        act = (_silu(a) * u).astype(BF16)
        part = jnp.dot(act, wd_ref[c * FF_CHUNK:(c + 1) * FF_CHUNK, :], preferred_element_type=F32)
        f = part if f is None else f + part
    y = _layer_norm(ALPHA * x + f, g_ref[...], b_ref[...])
    o_ref[...] = y
    ob_ref[...] = y.astype(BF16)


def _ffn(x, wu, wd, g, b, layer, tm):
    t = x.shape[0]
    tok = pl.BlockSpec((tm, D_MODEL), lambda i: (i, 0))
    return pl.pallas_call(
        _ffn_kernel,
        grid=(t // tm,),
        in_specs=[tok, _resident((D_MODEL, 2 * D_FF), layer), _resident((D_FF, D_MODEL), layer),
                  _resident((1, D_MODEL), layer), _resident((1, D_MODEL), layer)],
        out_specs=[tok, tok],
        out_shape=[jax.ShapeDtypeStruct((t, D_MODEL), F32),
                   jax.ShapeDtypeStruct((t, D_MODEL), BF16)],
        compiler_params=_cparams(("parallel",)),
        name="swiglu_ln",
    )(x, wu, wd, g, b)


def _pad_lanes(p):
    flat = p.reshape(DEPTH, 1, DT_COLS).astype(F32)
    return jnp.pad(flat, ((0, 0), (0, 0), (0, LANES - DT_COLS)))


def kernel(x_prompt, x_sample, w_in, conv_w, conv_b, a_log, dt_bias, d_skip, ssd_norm_g, lam_qk,
           attn_norm_g, w_ssd_out, w_attn_out, w_gate, b_gate, w_out, ln1_g, ln1_b, w_ffn_up,
           w_ffn_down, ln2_g, ln2_b):
    x = jnp.concatenate([x_prompt.reshape(T_PROMPT, D_MODEL),
                         x_sample.reshape(N_SAMPLE * L_SAMPLE, D_MODEL)], axis=0)
    xb = x.astype(BF16)

    inv = ROPE_THETA ** (-jnp.arange(0, QK_DIM, 2, dtype=F32) / QK_DIM)
    ang = jnp.arange(L_PROMPT, dtype=F32)[:, None] * inv[None, :]
    cos, sin = jnp.cos(ang), jnp.sin(ang)
    cos_t = jnp.concatenate([cos, cos, cos, cos], axis=-1)
    sin_t = jnp.concatenate([-sin, sin, -sin, sin], axis=-1)

    w_zx = w_in[:, :, :ZX_COLS].astype(BF16)
    w_dt = jnp.pad(w_in[:, :, ZX_COLS:QKV_OFF], ((0, 0), (0, 0), (0, LANES - DT_COLS))).astype(BF16)
    w_qkv = w_in[:, :, QKV_OFF:].astype(BF16)
    w_g = w_gate.astype(BF16)
    w_so = w_ssd_out.astype(BF16)
    w_ao = w_attn_out.astype(BF16)
    w_o = w_out.astype(BF16)
    w_up = w_ffn_up.astype(BF16)
    w_dn = w_ffn_down.astype(BF16)
    conv_w_p = jnp.pad(conv_w, ((0, 0), (0, SUBLANES - CONV_WIDTH), (0, 0)))
    conv_b_p = conv_b.reshape(DEPTH, 1, CONV_DIM)
    alog_p = _pad_lanes(a_log)
    dtb_p = _pad_lanes(dt_bias)
    dskip_p = jnp.repeat(d_skip, SSD_HEAD_DIM, axis=-1).reshape(DEPTH, 1, D_INNER)
    ng_p = ssd_norm_g.reshape(DEPTH, 1, D_INNER)
    ag_p = attn_norm_g.reshape(DEPTH, 1, V_DIM)
    row = lambda p: p.reshape(DEPTH, 1, -1)

    for i in range(DEPTH):
        lambda_init = 0.8 - 0.6 * math.exp(-0.3 * i)
        zx = _matmul(xb, w_zx, i, 1024, 2048, F32)
        prep = _ssd_prep(xb, w_dt, alog_p, dtb_p, i, 1024)
        q_t = _head_proj(xb, w_qkv, i, 0, 1024, rope_tables=(cos_t, sin_t),
                         scale=QK_SCALE * LOG2_E, transpose=True)
        k_r = _head_proj(xb, w_qkv, i, 1, 1024, rope_tables=(cos_t, sin_t))
        v_t = _head_proj(xb, w_qkv, i, 2, 1024, transpose=True, ones_rows=ONES_ROWS)
        xs = _conv(zx, conv_w_p, conv_b_p, i, D_INNER, D_INNER, F32, 512, 1024)
        bc = _conv(zx, conv_w_p, conv_b_p, i, 2 * D_INNER, BC_COLS, BF16, 512, 1024)
        y_f = _ssd(xs, bc, prep, i, reverse=False)
        ssd_y = _ssd(xs, bc, prep, i, reverse=True, extra=(y_f, zx, dskip_p, ng_p))
        att_p = _attention(q_t, k_r, v_t, lam_qk, ag_p, i, 0, 1, L_PROMPT, lambda_init,
                           256, 512, 16)
        att_s = _attention(q_t, k_r, v_t, lam_qk, ag_p, i, T_PROMPT, N_SAMPLE, L_SAMPLE,
                           lambda_init, 256, 512, 8)
        x, xb = _post(x, ssd_y, att_p, att_s, w_g, row(b_gate), w_so, w_ao, w_o,
                      row(ln1_g), row(ln1_b), i, 512)
        x, xb = _ffn(x, w_up, w_dn, row(ln2_g), row(ln2_b), i, 512)

    y_prompt = x[:T_PROMPT].reshape(1, L_PROMPT, D_MODEL)
    y_sample = x[T_PROMPT:].reshape(N_SAMPLE, L_SAMPLE, D_MODEL)
    return (y_prompt, y_sample)
```
